```python
import jax
import jax.numpy as jnp
from jax import lax
import numpy as np

D_MODEL = 1024
BATCH = 1
SEQ = 16384
DEPTH = 1
DEC_BATCH = 128
DEC_SEQ = 1
PAST_LEN = 8192
PAGE_SIZE = 128

D_MIX = D_MODEL
HEAD_DIM = 64
N_ATT_HEADS = (D_MIX // 2) // HEAD_DIM
D_ATT = N_ATT_HEADS * HEAD_DIM
N_SSD_HEADS = 8
D_SSD = D_MIX - D_ATT
SSD_HEAD_DIM = D_SSD // N_SSD_HEADS
N_SSD_GROUPS = 2
D_STATE = 128
CONV_W = 4
D_CONV = D_SSD + 2 * N_SSD_GROUPS * D_STATE
SSD_CHUNK = 128
Q_BLOCK = 128
D_FF = 128 * ((8 * D_MODEL // 3 + 127) // 128)
N_ADA = 9
EPS = 1e-6
FORGET_BIAS_MEAN = 3.0
SPLIT_POINTS = (D_ATT, 2 * D_ATT, 3 * D_ATT, 3 * D_ATT + N_ATT_HEADS,
                3 * D_ATT + N_ATT_HEADS + D_SSD, 3 * D_ATT + N_ATT_HEADS + D_SSD + D_CONV)
D_PROJ = 3 * D_ATT + N_ATT_HEADS + D_SSD + D_CONV + N_SSD_HEADS
F32 = jnp.float32

kernel_name = 'hybrid_fox_ssd_macaron_step'


def rms_norm(x, g):
    xf = x.astype(F32)
    xf = xf * lax.rsqrt(jnp.mean(xf * xf, axis=-1, keepdims=True) + EPS)
    return xf.astype(x.dtype) * g


def ada_params(c, w_ada, b_ada):
    m = jax.nn.silu(c) @ w_ada + b_ada
    return jnp.split(m[:, None, :], N_ADA, axis=-1)


def modulate(h, shift, scale):
    return h * (1.0 + scale) + shift


def swiglu(h, w_in, w_out):
    g, u = jnp.split(h @ w_in, 2, axis=-1)
    return (jax.nn.silu(g) * u) @ w_out


def causal_conv(xpad, w, b):
    t = xpad.shape[1] - (CONV_W - 1)
    out = b
    for i in range(CONV_W):
        out = out + xpad[:, i:i + t] * w[i]
    return out


def mixer_inputs(h, conv_buf, w_in, b_forget, conv_w, conv_b, dt_bias, a_log):
    bsz, t, _ = h.shape
    q, k, v, f_raw, z, xbc, dt_raw = jnp.split(h @ w_in, SPLIT_POINTS, axis=-1)
    heads = (bsz, t, N_ATT_HEADS, HEAD_DIM)
    logf = jax.nn.log_sigmoid((f_raw + b_forget).astype(F32))
    xpad = jnp.concatenate([conv_buf.astype(xbc.dtype), xbc], axis=1)
    xc = jax.nn.silu(causal_conv(xpad, conv_w, conv_b))
    xs, b_in, c_in = jnp.split(xc, (D_SSD, D_SSD + N_SSD_GROUPS * D_STATE), axis=-1)
    rep = N_SSD_HEADS // N_SSD_GROUPS
    xs = xs.reshape(bsz, t, N_SSD_HEADS, SSD_HEAD_DIM).astype(F32)
    b_in = jnp.repeat(b_in.reshape(bsz, t, N_SSD_GROUPS, D_STATE), rep, axis=2).astype(F32)
    c_in = jnp.repeat(c_in.reshape(bsz, t, N_SSD_GROUPS, D_STATE), rep, axis=2).astype(F32)
    dt = jax.nn.softplus((dt_raw + dt_bias).astype(F32))
    a = -jnp.exp(a_log.astype(F32))
    return (q.reshape(heads), k.reshape(heads), v.reshape(heads), logf, z, xs, b_in, c_in, dt, a,
            xpad[:, -(CONV_W - 1):])


def fox_prompt(q, k, v, logf):
    bsz, t, h, d = q.shape
    nb = t // Q_BLOCK
    scale = d ** -0.5
    f_cum = jnp.cumsum(logf, axis=1)
    f_keys = jnp.transpose(f_cum, (0, 2, 1))
    kpos = jnp.arange(t)
    q_blocks = jnp.moveaxis(q.reshape(bsz, nb, Q_BLOCK, h, d), 1, 0)
    f_blocks = jnp.moveaxis(f_cum.reshape(bsz, nb, Q_BLOCK, h), 1, 0)

    def one_block(args):
        i, q_i, f_i = args
        s = jnp.einsum('bqhd,bkhd->bhqk', q_i, k).astype(F32) * scale
        s = s + jnp.transpose(f_i, (0, 2, 1))[..., :, None] - f_keys[..., None, :]
        qpos = i * Q_BLOCK + jnp.arange(Q_BLOCK)
        s = jnp.where(kpos[None, :] <= qpos[:, None], s, -jnp.inf)
        p = jax.nn.softmax(s, axis=-1).astype(v.dtype)
        return jnp.einsum('bhqk,bkhd->bqhd', p, v)

    o = lax.map(one_block, (jnp.arange(nb), q_blocks, f_blocks))
    return jnp.moveaxis(o, 0, 1).reshape(bsz, t, h, d)


def fox_sample(q, k, v, logf, cache_k, cache_v, cache_logf, page_table, layer):
    s_new = q.shape[1]
    past = page_table.shape[1] * PAGE_SIZE
    scale = HEAD_DIM ** -0.5
    kpos = jnp.arange(past + s_new)
    qpos = past + jnp.arange(s_new)
    mask = kpos[None, :] <= qpos[:, None]

    def one_sequence(args):
        q_b, k_b, v_b, lf_b, pt = args
        k_all = jnp.concatenate([cache_k[layer, pt].reshape(past, N_ATT_HEADS, HEAD_DIM), k_b], axis=0)
        v_all = jnp.concatenate([cache_v[layer, pt].reshape(past, N_ATT_HEADS, HEAD_DIM), v_b], axis=0)
        lf_all = jnp.concatenate([cache_logf[layer, pt].reshape(past, N_ATT_HEADS).astype(F32), lf_b], axis=0)
        f_cum = jnp.cumsum(lf_all, axis=0).T
        s = jnp.einsum('qhd,khd->hqk', q_b, k_all).astype(F32) * scale
        s = s + f_cum[:, past:, None] - f_cum[:, None, :]
        s = jnp.where(mask, s, -jnp.inf)
        p = jax.nn.softmax(s, axis=-1).astype(v_all.dtype)
        return jnp.einsum('hqk,khd->qhd', p, v_all)

    return lax.map(one_sequence, (q, k, v, logf, page_table))


def ssd_chunked(xs, dt, a, b_in, c_in, h0):
    bsz, t, h, p = xs.shape
    nc = t // SSD_CHUNK
    ln = SSD_CHUNK
    xdt = (xs * dt[..., None]).reshape(bsz, nc, ln, h, p)
    bc = b_in.reshape(bsz, nc, ln, h, D_STATE)
    cc = c_in.reshape(bsz, nc, ln, h, D_STATE)
    a_cum = jnp.cumsum((dt * a).reshape(bsz, nc, ln, h), axis=2)
    causal = jnp.tril(jnp.ones((ln, ln), bool))[None, None, :, :, None]
    seg = a_cum[:, :, :, None, :] - a_cum[:, :, None, :, :]
    decay = jnp.exp(jnp.where(causal, seg, -jnp.inf))
    scores = jnp.einsum('bclhn,bcshn->bclsh', cc, bc) * decay
    y_diag = jnp.einsum('bclsh,bcshp->bclhp', scores, xdt)
    to_end = jnp.exp(a_cum[:, :, -1:, :] - a_cum)
    chunk_states = jnp.einsum('bclhn,bclhp->bchpn', bc * to_end[..., None], xdt)
    chunk_decay = jnp.exp(a_cum[:, :, -1, :])

    def carry_state(hs, inp):
        st, dec = inp
        return hs * dec[..., None, None] + st, hs

    h_last, h_in = lax.scan(carry_state, h0, (jnp.moveaxis(chunk_states, 1, 0), jnp.moveaxis(chunk_decay, 1, 0)))
    h_in = jnp.moveaxis(h_in, 0, 1)
    y_off = jnp.einsum('bclhn,bchpn->bclhp', cc, h_in) * jnp.exp(a_cum)[..., None]
    return (y_diag + y_off).reshape(bsz, t, h, p), h_last


def ssd_recurrent(xs, dt, a, b_in, c_in, h0):
    def step(hs, inp):
        x_t, dt_t, b_t, c_t = inp
        hs = hs * jnp.exp(dt_t * a)[..., None, None] + jnp.einsum('bhp,bhn->bhpn', x_t * dt_t[..., None], b_t)
        return hs, jnp.einsum('bhn,bhpn->bhp', c_t, hs)

    h_last, ys = lax.scan(step, h0, (jnp.moveaxis(xs, 1, 0), jnp.moveaxis(dt, 1, 0),
                                     jnp.moveaxis(b_in, 1, 0), jnp.moveaxis(c_in, 1, 0)))
    return jnp.moveaxis(ys, 0, 1), h_last


def mixer_output(att, y_ssd, xs, z, d_skip, ssd_norm, w_out):
    bsz, t = att.shape[:2]
    y = y_ssd + d_skip.astype(F32)[:, None] * xs
    y = y.reshape(bsz, t, D_SSD) * jax.nn.silu(z.astype(F32))
    yg = y.reshape(bsz, t, N_SSD_GROUPS, D_SSD // N_SSD_GROUPS)
    yg = yg * lax.rsqrt(jnp.mean(yg * yg, axis=-1, keepdims=True) + EPS)
    y = yg.reshape(bsz, t, D_SSD).astype(att.dtype) * ssd_norm
    return jnp.concatenate([att.reshape(bsz, t, D_ATT), y], axis=-1) @ w_out


def macaron_layer(x, c, mixer_fn, w_ada, b_ada, norm_ffn1, w_ffn1_in, w_ffn1_out, norm_mix,
                  norm_ffn2, w_ffn2_in, w_ffn2_out):
    sh1, sc1, g1, sh2, sc2, g2, sh3, sc3, g3 = ada_params(c, w_ada, b_ada)
    x = x + 0.5 * g1 * swiglu(modulate(rms_norm(x, norm_ffn1), sh1, sc1), w_ffn1_in, w_ffn1_out)
    mix, state = mixer_fn(modulate(rms_norm(x, norm_mix), sh2, sc2))
    x = x + g2 * mix
    x = x + 0.5 * g3 * swiglu(modulate(rms_norm(x, norm_ffn2), sh3, sc3), w_ffn2_in, w_ffn2_out)
    return x, state


def setup_inputs(seed: int = 0) -> dict:
    key = jax.random.key(seed)
    counter = [0]

    def nk():
        counter[0] += 1
        return jax.random.fold_in(key, counter[0])

    def nrm(shape, scale=1.0):
        return scale * jax.random.normal(nk(), shape, F32)

    def gain(shape):
        return 1.0 + nrm(shape, 0.02)

    n_pages = PAST_LEN // PAGE_SIZE
    n_phys = (DEC_BATCH * n_pages * 5) // 4
    L = DEPTH
    x_prompt = nrm((BATCH, SEQ, D_MODEL))
    x_sample = nrm((DEC_BATCH, DEC_SEQ, D_MODEL))
    cache_k = nrm((L, n_phys, PAGE_SIZE, N_ATT_HEADS, HEAD_DIM))
    cache_v = nrm((L, n_phys, PAGE_SIZE, N_ATT_HEADS, HEAD_DIM))
    cache_logf = jax.nn.log_sigmoid(FORGET_BIAS_MEAN + nrm((L, n_phys, PAGE_SIZE, N_ATT_HEADS)))
    state_ssm = nrm((L, DEC_BATCH, N_SSD_HEADS, SSD_HEAD_DIM, D_STATE), 0.1)
    state_conv = nrm((L, DEC_BATCH, CONV_W - 1, D_CONV))
    page_table = jax.random.permutation(nk(), n_phys)[: DEC_BATCH * n_pages].reshape(DEC_BATCH, n_pages).astype(jnp.int32)
    c_prompt = nrm((BATCH, D_MODEL))
    c_sample = nrm((DEC_BATCH, D_MODEL))
    w_ada = nrm((L, D_MODEL, N_ADA * D_MODEL), D_MODEL ** -0.5)
    b_ada = nrm((L, N_ADA * D_MODEL), 0.01)
    norm_ffn1 = gain((L, D_MODEL))
    w_ffn1_in = nrm((L, D_MODEL, 2 * D_FF), D_MODEL ** -0.5)
    w_ffn1_out = nrm((L, D_FF, D_MODEL), D_FF ** -0.5)
    norm_mix = gain((L, D_MODEL))
    w_in = nrm((L, D_MODEL, D_PROJ), D_MODEL ** -0.5)
    b_forget = FORGET_BIAS_MEAN + nrm((L, N_ATT_HEADS), 0.5)
    conv_w = nrm((L, CONV_W, D_CONV), CONV_W ** -0.5)
    conv_b = nrm((L, D_CONV), 0.01)
    dt0 = jax.random.uniform(nk(), (L, N_SSD_HEADS), F32, 1e-3, 1e-1)
    dt_bias = dt0 + jnp.log(-jnp.expm1(-dt0))
    a_log = jnp.log(jax.random.uniform(nk(), (L, N_SSD_HEADS), F32, 1.0, 16.0))
    d_skip = 1.0 + nrm((L, N_SSD_HEADS), 0.1)
    ssd_norm = gain((L, D_SSD))
    w_out = nrm((L, D_MIX, D_MODEL), D_MIX ** -0.5)
    norm_ffn2 = gain((L, D_MODEL))
    w_ffn2_in = nrm((L, D_MODEL, 2 * D_FF), D_MODEL ** -0.5)
    w_ffn2_out = nrm((L, D_FF, D_MODEL), D_FF ** -0.5)
    norm_final = gain((D_MODEL,))
    return {'x_prompt': x_prompt, 'x_sample': x_sample, 'cache_k': cache_k, 'cache_v': cache_v,
            'cache_logf': cache_logf, 'state_ssm': state_ssm, 'state_conv': state_conv,
            'page_table': page_table, 'c_prompt': c_prompt, 'c_sample': c_sample,
            'w_ada': w_ada, 'b_ada': b_ada, 'norm_ffn1': norm_ffn1, 'w_ffn1_in': w_ffn1_in,
            'w_ffn1_out': w_ffn1_out, 'norm_mix': norm_mix, 'w_in': w_in, 'b_forget': b_forget,
            'conv_w': conv_w, 'conv_b': conv_b, 'dt_bias': dt_bias, 'a_log': a_log, 'd_skip': d_skip,
            'ssd_norm': ssd_norm, 'w_out': w_out, 'norm_ffn2': norm_ffn2, 'w_ffn2_in': w_ffn2_in,
            'w_ffn2_out': w_ffn2_out, 'norm_final': norm_final}


def reference(x_prompt, x_sample, cache_k, cache_v, cache_logf, state_ssm, state_conv, page_table,
              c_prompt, c_sample, w_ada, b_ada, norm_ffn1, w_ffn1_in, w_ffn1_out, norm_mix, w_in,
              b_forget, conv_w, conv_b, dt_bias, a_log, d_skip, ssd_norm, w_out, norm_ffn2,
              w_ffn2_in, w_ffn2_out, norm_final):
    xp, xsm = x_prompt, x_sample
    new_p, new_s = [], []
    for l in range(DEPTH):
        mix_w = (w_in[l], b_forget[l], conv_w[l], conv_b[l], dt_bias[l], a_log[l])
        out_w = (d_skip[l], ssd_norm[l], w_out[l])
        blk_w = (w_ada[l], b_ada[l], norm_ffn1[l], w_ffn1_in[l], w_ffn1_out[l], norm_mix[l],
                 norm_ffn2[l], w_ffn2_in[l], w_ffn2_out[l])

        def mix_prompt(h):
            bsz = h.shape[0]
            conv0 = jnp.zeros((bsz, CONV_W - 1, D_CONV), h.dtype)
            q, k, v, logf, z, xs, b_in, c_in, dt, a, conv_new = mixer_inputs(h, conv0, *mix_w)
            att = fox_prompt(q, k, v, logf)
            h0 = jnp.zeros((bsz, N_SSD_HEADS, SSD_HEAD_DIM, D_STATE), F32)
            y_ssd, h_last = ssd_chunked(xs, dt, a, b_in, c_in, h0)
            out = mixer_output(att, y_ssd, xs, z, *out_w)
            return out, (k, v, logf.astype(cache_logf.dtype), h_last.astype(state_ssm.dtype), conv_new)

        def mix_sample(h):
            q, k, v, logf, z, xs, b_in, c_in, dt, a, conv_new = mixer_inputs(h, state_conv[l], *mix_w)
            att = fox_sample(q, k, v, logf, cache_k, cache_v, cache_logf, page_table, l)
            y_ssd, h_last = ssd_recurrent(xs, dt, a, b_in, c_in, state_ssm[l].astype(F32))
            out = mixer_output(att, y_ssd, xs, z, *out_w)
            return out, (k, v, logf.astype(cache_logf.dtype), h_last.astype(state_ssm.dtype), conv_new)

        xp, st_p = macaron_layer(xp, c_prompt, mix_prompt, *blk_w)
        xsm, st_s = macaron_layer(xsm, c_sample, mix_sample, *blk_w)
        new_p.append(st_p)
        new_s.append(st_s)

    y_prompt = rms_norm(xp, norm_final)
    y_sample = rms_norm(xsm, norm_final)
    k_prompt = jnp.stack([s[0] for s in new_p])
    v_prompt = jnp.stack([s[1] for s in new_p])
    logf_prompt = jnp.stack([s[2] for s in new_p])
    ssm_prompt = jnp.stack([s[3] for s in new_p])
    conv_prompt = jnp.stack([s[4] for s in new_p])
    k_sample = jnp.stack([s[0] for s in new_s])
    v_sample = jnp.stack([s[1] for s in new_s])
    logf_sample = jnp.stack([s[2] for s in new_s])
    ssm_sample = jnp.stack([s[3] for s in new_s])
    conv_sample = jnp.stack([s[4] for s in new_s])
    return (y_prompt, y_sample, k_prompt, v_prompt, logf_prompt, ssm_prompt, conv_prompt,
            k_sample, v_sample, logf_sample, ssm_sample, conv_sample)
```

```python
import functools

import jax
import jax.numpy as jnp
from jax import lax
from jax.experimental import pallas as pl
from jax.experimental.pallas import tpu as pltpu

F32 = jnp.float32
BF16 = jnp.bfloat16
EPS = 1e-6
N_ADA = 9
LANES = 128
SUBLANES = 8
SSD_CHUNK = 128
VMEM_LIMIT = 56 * 1024 * 1024


def _cparams(*semantics):
    return pltpu.CompilerParams(dimension_semantics=semantics, vmem_limit_bytes=VMEM_LIMIT)


def _dot(a, b):
    return jnp.dot(a.astype(BF16), b.astype(BF16), preferred_element_type=F32)


def _dot_nt(a, b):
    return lax.dot_general(a.astype(BF16), b.astype(BF16), (((1,), (1,)), ((), ())),
                           preferred_element_type=F32)


def _split3(a):
    hi = a.astype(BF16)
    r = a - hi.astype(F32)
    mid = r.astype(BF16)
    lo = (r - mid.astype(F32)).astype(BF16)
    return hi, mid, lo


def _xdot_l(a, e):
    hi, mid, lo = _split3(a)
    f = functools.partial(jnp.dot, preferred_element_type=F32)
    return f(hi, e) + f(mid, e) + f(lo, e)


def _xdot_r(e, b):
    hi, mid, lo = _split3(b)
    f = functools.partial(jnp.dot, preferred_element_type=F32)
    return f(e, hi) + f(e, mid) + f(e, lo)


def _silu(x):
    return x * jax.nn.sigmoid(x)


def _softplus(x):
    return jnp.maximum(x, 0.0) + jnp.log1p(jnp.exp(-jnp.abs(x)))


def _rms(x):
    return x * lax.rsqrt(jnp.mean(x * x, axis=-1, keepdims=True) + EPS)


def _norm_mod(x, nw, shift, scale):
    return (_rms(x) * nw) * (1.0 + scale) + shift


def _expand_matrix(n_rows, n_heads, width):
    row = lax.broadcasted_iota(jnp.int32, (n_rows, n_heads * width), 0)
    col = lax.broadcasted_iota(jnp.int32, (n_rows, n_heads * width), 1)
    return (col // width == row).astype(BF16)


def _tri_matrix(n):
    row = lax.broadcasted_iota(jnp.int32, (n, n), 0)
    col = lax.broadcasted_iota(jnp.int32, (n, n), 1)
    return (col <= row).astype(BF16)


def _gated_group_norm(y, xs, z, dskip, gnorm, n_groups):
    y = (y + dskip * xs) * _silu(z)
    gw = y.shape[1] // n_groups
    parts = [_rms(y[:, g * gw:(g + 1) * gw]) for g in range(n_groups)]
    return (jnp.concatenate(parts, axis=1) * gnorm).astype(BF16)


def _ada_kernel(c_ref, w_ref, b_ref, o_ref):
    o_ref[...] = _dot(_silu(c_ref[...]), w_ref[...]) + b_ref[...]


def _ada_call(c_all, w, b):
    r, d = c_all.shape
    n = w.shape[1]
    tn = d
    return pl.pallas_call(
        _ada_kernel,
        grid=(n // tn,),
        in_specs=[pl.BlockSpec((r, d), lambda j: (0, 0)),
                  pl.BlockSpec((d, tn), lambda j: (0, j)),
                  pl.BlockSpec((1, tn), lambda j: (0, j))],
        out_specs=pl.BlockSpec((r, tn), lambda j: (0, j)),
        out_shape=jax.ShapeDtypeStruct((r, n), F32),
        compiler_params=_cparams("parallel"),
        name="ada",
    )(c_all, w, b.reshape(1, n))


def _mod_spec(per_row, tm, d, idx, prompt_row_block):
    if per_row:
        return pl.BlockSpec((tm, d), lambda i: (i, idx))
    return pl.BlockSpec((SUBLANES, d), lambda i: (prompt_row_block, idx))


def _ffn_kernel(*refs, per_row, with_mix, final_norm, ff, tf, datt):
    it = iter(refs)
    x_ref = next(it)
    if with_mix:
        att_ref, ys_ref, wo_ref, g2_ref = next(it), next(it), next(it), next(it)
    sh_ref, sc_ref, g_ref, nw_ref, win_ref, wout_ref = (next(it) for _ in range(6))
    nf_ref = next(it) if final_norm else None
    o_ref = next(it)
    a_ref = next(it)
    rows = slice(None) if per_row else slice(0, 1)

    x = x_ref[...]
    if with_mix:
        mix = (jnp.dot(att_ref[...].astype(BF16), wo_ref[0:datt, :], preferred_element_type=F32)
               + jnp.dot(ys_ref[...], wo_ref[datt:, :], preferred_element_type=F32))
        x = x + g2_ref[rows, :] * mix
    h = _norm_mod(x, nw_ref[...], sh_ref[rows, :], sc_ref[rows, :]).astype(BF16)
    for c in range(ff // tf):
        g = jnp.dot(h, win_ref[:, c * tf:(c + 1) * tf], preferred_element_type=F32)
        u = jnp.dot(h, win_ref[:, ff + c * tf:ff + (c + 1) * tf], preferred_element_type=F32)
        a_ref[:, c * tf:(c + 1) * tf] = (_silu(g) * u).astype(BF16)
    y = x + (0.5 * g_ref[rows, :]) * jnp.dot(a_ref[...], wout_ref[...], preferred_element_type=F32)
    if final_norm:
        y = _rms(y) * nf_ref[...]
    o_ref[...] = y


def _ffn_call(x, m_all, ada_base, norm_w, w_in_b, w_out_b, *, per_row, tm, prompt_row_block,
              mix=None, final_norm_w=None, name):
    r, d = x.shape
    ff = w_out_b.shape[0]
    tf = 256
    assert r % tm == 0 and ff % tf == 0
    const = lambda i: (0, 0)
    resident = functools.partial(pl.BlockSpec, index_map=const, pipeline_mode=pl.Buffered(1))
    mspec = functools.partial(_mod_spec, per_row, tm, d, prompt_row_block=prompt_row_block)
    args, specs = [x], [pl.BlockSpec((tm, d), lambda i: (i, 0))]
    datt = 0
    if mix is not None:
        att, ys, wo_b, g2_idx = mix
        datt = att.shape[1]
        args += [att, ys, wo_b, m_all]
        specs += [pl.BlockSpec((tm, datt), lambda i: (i, 0)),
                  pl.BlockSpec((tm, ys.shape[1]), lambda i: (i, 0)),
                  resident(wo_b.shape), mspec(g2_idx)]
    args += [m_all, m_all, m_all, norm_w.reshape(1, d), w_in_b, w_out_b]
    specs += [mspec(ada_base), mspec(ada_base + 1), mspec(ada_base + 2),
              pl.BlockSpec((1, d), const), resident(w_in_b.shape), resident(w_out_b.shape)]
    if final_norm_w is not None:
        args.append(final_norm_w.reshape(1, d))
        specs.append(pl.BlockSpec((1, d), const))
    kern = functools.partial(_ffn_kernel, per_row=per_row, with_mix=mix is not None,
                             final_norm=final_norm_w is not None, ff=ff, tf=tf, datt=datt)
    return pl.pallas_call(
        kern,
        grid=(r // tm,),
        in_specs=specs,
        out_specs=pl.BlockSpec((tm, d), lambda i: (i, 0)),
        out_shape=jax.ShapeDtypeStruct((r, d), F32),
        scratch_shapes=[pltpu.VMEM((tm, ff), BF16)],
        compiler_params=_cparams("parallel"),
        name=name,
    )(*args)


def _project(x, sh, sc, nw_ref, wqkv_ref, wzx_ref, wfd_ref, bfd_ref, *, datt, dssd, nh, nhs):
    h = _norm_mod(x, nw_ref[...], sh, sc).astype(BF16)
    qkv = jnp.dot(h, wqkv_ref[...], preferred_element_type=F32)
    zx = jnp.dot(h, wzx_ref[...], preferred_element_type=F32)
    fd = jnp.dot(h, wfd_ref[...], preferred_element_type=F32) + bfd_ref[...]
    lane = lax.broadcasted_iota(jnp.int32, (1, LANES), 1)
    logf = jnp.where(lane < nh, -_softplus(-fd[:, :LANES]), 0.0)
    dt = jnp.where(lane < nhs, _softplus(fd[:, LANES:]), 0.0)
    return (qkv[:, :datt], qkv[:, datt:2 * datt], qkv[:, 2 * datt:], zx[:, :dssd], zx[:, dssd:],
            logf, dt)


def _inproj_prompt_kernel(x_ref, sh_ref, sc_ref, nw_ref, wqkv_ref, wzx_ref, wfd_ref, bfd_ref,
                          cw_ref, cb_ref,
                          q_ref, kt_ref, vt_ref, kb_ref, vb_ref, lft_ref, fcol_ref, frow_ref, z_ref,
                          xc_ref, dt_ref, tail_ref,
                          ptail_ref, cf_ref, *, datt, dssd, nh, nhs, scale, conv_w):
    @pl.when(pl.program_id(0) == 0)
    def _():
        ptail_ref[...] = jnp.zeros_like(ptail_ref)
        cf_ref[...] = jnp.zeros_like(cf_ref)

    tm = x_ref.shape[0]
    q, k, v, z, xbc, logf, dt = _project(
        x_ref[...], sh_ref[0:1, :], sc_ref[0:1, :], nw_ref, wqkv_ref, wzx_ref, wfd_ref, bfd_ref,
        datt=datt, dssd=dssd, nh=nh, nhs=nhs)
    q_ref[...] = (q * scale).astype(BF16)
    kt_ref[...] = k.T
    vt_ref[...] = v.T
    kb_ref[...] = k.astype(BF16)
    vb_ref[...] = v.astype(BF16)
    z_ref[...] = z
    lft_ref[...] = logf.T[0:SUBLANES, :]
    dt_ref[...] = dt

    fc = _xdot_r(_tri_matrix(tm), logf) + cf_ref[...]
    cf_ref[...] = fc[tm - 1:tm, :]
    fcol_ref[...] = fc
    frow_ref[...] = fc.T[0:SUBLANES, :]

    ptail = ptail_ref[...]
    row8 = lax.broadcasted_iota(jnp.int32, ptail.shape, 0)
    acc = cb_ref[...]
    for i in range(conv_w):
        s = conv_w - 1 - i
        if s == 0:
            xs = xbc
        else:
            r = pltpu.roll(xbc, s, 0)
            top = jnp.where(row8 < s, pltpu.roll(ptail, s, 0), r[0:SUBLANES])
            xs = jnp.concatenate([top, r[SUBLANES:]], axis=0)
        acc = acc + xs * cw_ref[i:i + 1, :]
    xc_ref[...] = _silu(acc)
    tail = xbc[tm - SUBLANES:tm]
    ptail_ref[...] = tail
    tail_ref[...] = tail


def _inproj_prompt_call(x, m_all, prompt_row_block, norm_w, wqkv, wzx, wfd, bfd, cw, cb, *,
                        datt, dssd, nh, nhs, scale, tm):
    t, d = x.shape
    dconv = cw.shape[1]
    assert t % tm == 0 and nh == SUBLANES and conv_w_ok(cw.shape[0])
    const = lambda i: (0, 0)
    rowblk = lambda w: pl.BlockSpec((tm, w), lambda i: (i, 0))
    colblk = lambda h: pl.BlockSpec((h, tm), lambda i: (0, i))
    mspec = functools.partial(_mod_spec, False, tm, d, prompt_row_block=prompt_row_block)
    kern = functools.partial(_inproj_prompt_kernel, datt=datt, dssd=dssd, nh=nh, nhs=nhs,
                             scale=scale, conv_w=cw.shape[0])
    sds = jax.ShapeDtypeStruct
    return pl.pallas_call(
        kern,
        grid=(t // tm,),
        in_specs=[rowblk(d), mspec(3), mspec(4), pl.BlockSpec((1, d), const),
                  pl.BlockSpec(wqkv.shape, const), pl.BlockSpec(wzx.shape, const),
                  pl.BlockSpec(wfd.shape, const), pl.BlockSpec(bfd.shape, const),
                  pl.BlockSpec(cw.shape, const), pl.BlockSpec((1, dconv), const)],
        out_specs=[rowblk(datt), colblk(datt), colblk(datt), rowblk(datt), rowblk(datt),
                   colblk(SUBLANES), rowblk(LANES), colblk(SUBLANES),
                   rowblk(dssd), rowblk(dconv), rowblk(LANES),
                   pl.BlockSpec((SUBLANES, dconv), const)],
        out_shape=[sds((t, datt), BF16), sds((datt, t), F32), sds((datt, t), F32),
                   sds((t, datt), BF16), sds((t, datt), BF16), sds((SUBLANES, t), F32),
                   sds((t, LANES), F32), sds((SUBLANES, t), F32), sds((t, dssd), F32),
                   sds((t, dconv), F32), sds((t, LANES), F32), sds((SUBLANES, dconv), F32)],
        scratch_shapes=[pltpu.VMEM((SUBLANES, dconv), F32), pltpu.VMEM((1, LANES), F32)],
        compiler_params=_cparams("arbitrary"),
        name="inproj_prompt",
    )(x, m_all, m_all, norm_w.reshape(1, d), wqkv, wzx, wfd, bfd, cw, cb.reshape(1, dconv))


def conv_w_ok(w):
    return 1 <= w - 1 < SUBLANES


def _inproj_sample_kernel(x_ref, sh_ref, sc_ref, nw_ref, wqkv_ref, wzx_ref, wfd_ref, bfd_ref,
                          cw_ref, cb_ref, cs_ref,
                          q_ref, k_ref, v_ref, kt_ref, vt_ref, lf_ref, lft_ref, z_ref, xc_ref, dt_ref,
                          cn_ref, *, datt, dssd, nh, nhs, scale, conv_w):
    q, k, v, z, xbc, logf, dt = _project(
        x_ref[...], sh_ref[...], sc_ref[...], nw_ref, wqkv_ref, wzx_ref, wfd_ref, bfd_ref,
        datt=datt, dssd=dssd, nh=nh, nhs=nhs)
    q_ref[...] = q * scale
    k_ref[...] = k
    v_ref[...] = v
    kt_ref[...] = k.T
    vt_ref[...] = v.T
    z_ref[...] = z
    lf_ref[...] = logf[:, 0:nh]
    lft_ref[...] = logf.T[0:SUBLANES, :]
    dt_ref[...] = dt
    acc = cb_ref[...]
    for i in range(conv_w - 1):
        acc = acc + cs_ref[i] * cw_ref[i:i + 1, :]
    acc = acc + xbc * cw_ref[conv_w - 1:conv_w, :]
    xc_ref[...] = _silu(acc)
    for i in range(conv_w - 2):
        cn_ref[i] = cs_ref[i + 1]
    cn_ref[conv_w - 2] = xbc


def _inproj_sample_call(x, m_all, norm_w, wqkv, wzx, wfd, bfd, cw, cb, conv_state, *,
                        datt, dssd, nh, nhs, scale):
    r, d = x.shape
    conv_w, dconv = cw.shape
    const = lambda i: (0, 0)
    full = lambda a: pl.BlockSpec(a.shape, lambda i: (0,) * a.ndim)
    mspec = functools.partial(_mod_spec, True, r, d, prompt_row_block=0)
    kern = functools.partial(_inproj_sample_kernel, datt=datt, dssd=dssd, nh=nh, nhs=nhs,
                             scale=scale, conv_w=conv_w)
    sds = jax.ShapeDtypeStruct
    shapes = [sds((r, datt), F32), sds((r, datt), F32), sds((r, datt), F32), sds((datt, r), F32),
              sds((datt, r), F32), sds((r, nh), F32), sds((SUBLANES, r), F32), sds((r, dssd), F32),
              sds((r, dconv), F32), sds((r, LANES), F32), sds(conv_state.shape, F32)]
    return pl.pallas_call(
        kern,
        grid=(1,),
        in_specs=[full(x), mspec(3), mspec(4), pl.BlockSpec((1, d), const), full(wqkv), full(wzx),
                  full(wfd), full(bfd), full(cw), pl.BlockSpec((1, dconv), const), full(conv_state)],
        out_specs=[full(s) for s in shapes],
        out_shape=shapes,
        compiler_params=_cparams("arbitrary"),
        name="inproj_sample",
    )(x, m_all, m_all, norm_w.reshape(1, d), wqkv, wzx, wfd, bfd, cw, cb.reshape(1, dconv), conv_state)


def _attn_prompt_kernel(q_ref, k_ref, v_ref, fq_ref, fk_ref, o_ref, m_ref, l_ref, acc_ref, *, nh, hd):
    i = pl.program_id(0)
    j = pl.program_id(1)
    tq, tk = q_ref.shape[0], k_ref.shape[0]
    per = LANES // hd

    @pl.when(j == 0)
    def _():
        m_ref[...] = jnp.full_like(m_ref, -jnp.inf)
        l_ref[...] = jnp.zeros_like(l_ref)
        acc_ref[...] = jnp.zeros_like(acc_ref)

    lane = lax.broadcasted_iota(jnp.int32, (1, LANES), 1)

    def step(masked):
        if masked:
            keep = (lax.broadcasted_iota(jnp.int32, (tq, tk), 1)
                    <= lax.broadcasted_iota(jnp.int32, (tq, tk), 0))
        for g in range(nh // per):
            sl = slice(g * LANES, (g + 1) * LANES)
            qg, kg, vg = q_ref[:, sl], k_ref[:, sl], v_ref[:, sl]
            acc_old = acc_ref[:, sl]
            acc_new = acc_old
            for hh in range(per):
                h = g * per + hh
                in_h = lane // hd == hh
                s = _dot_nt(jnp.where(in_h, qg, jnp.zeros_like(qg)), kg)
                s = s + fq_ref[:, h:h + 1] - fk_ref[h:h + 1, :]
                if masked:
                    s = jnp.where(keep, s, -jnp.inf)
                m_old = m_ref[h]
                m_new = jnp.maximum(m_old, jnp.max(s, axis=1, keepdims=True))
                alpha = jnp.exp(m_old - m_new)
                p = jnp.exp(s - m_new)
                l_ref[h] = alpha * l_ref[h] + jnp.sum(p, axis=1, keepdims=True)
                m_ref[h] = m_new
                pv = jnp.dot(p.astype(BF16), vg, preferred_element_type=F32)
                acc_new = jnp.where(in_h, alpha * acc_old + pv, acc_new)
            acc_ref[:, sl] = acc_new

    @pl.when(j < i)
    def _():
        step(False)

    @pl.when(j == i)
    def _():
        step(True)
        for g in range(nh // per):
            sl = slice(g * LANES, (g + 1) * LANES)
            inv = jnp.zeros((tq, LANES), F32)
            for hh in range(per):
                inv = jnp.where(lane // hd == hh, 1.0 / l_ref[g * per + hh], inv)
            o_ref[:, sl] = (acc_ref[:, sl] * inv).astype(BF16)


def _attn_prompt_call(q, kb, vb, fcol, frow, *, nh, hd, tq):
    t, datt = q.shape
    assert t % tq == 0 and LANES % hd == 0 and nh % (LANES // hd) == 0
    nq = t // tq
    kern = functools.partial(_attn_prompt_kernel, nh=nh, hd=hd)
    return pl.pallas_call(
        kern,
        grid=(nq, nq),
        in_specs=[pl.BlockSpec((tq, datt), lambda i, j: (i, 0)),
                  pl.BlockSpec((tq, datt), lambda i, j: (jnp.minimum(i, j), 0)),
                  pl.BlockSpec((tq, datt), lambda i, j: (jnp.minimum(i, j), 0)),
                  pl.BlockSpec((tq, LANES), lambda i, j: (i, 0)),
                  pl.BlockSpec((SUBLANES, tq), lambda i, j: (0, jnp.minimum(i, j)))],
        out_specs=pl.BlockSpec((tq, datt), lambda i, j: (i, 0)),
        out_shape=jax.ShapeDtypeStruct((t, datt), BF16),
        scratch_shapes=[pltpu.VMEM((nh, tq, 1), F32), pltpu.VMEM((nh, tq, 1), F32),
                        pltpu.VMEM((tq, datt), F32)],
        compiler_params=_cparams("parallel", "arbitrary"),
        name="attn_prompt",
    )(q, kb, vb, fcol, frow)


def _attn_sample_kernel(pt_ref, q_ref, kn_ref, vn_ref, lfn_ref, *rest, nh, hd, pg):
    k_refs, v_refs, lf_refs = rest[0:pg], rest[pg:2 * pg], rest[2 * pg:3 * pg]
    o_ref, qblk_ref, m_ref, l_ref, acc_ref, cf_ref = rest[3 * pg:]
    j = pl.program_id(1)
    datt, page = k_refs[0].shape

    def head_rows(a):
        return jnp.concatenate([jnp.broadcast_to(a[h:h + 1, :], (hd, a.shape[1])) for h in range(nh)],
                               axis=0)

    @pl.when(j == 0)
    def _():
        row = lax.broadcasted_iota(jnp.int32, (nh, datt), 0)
        col = lax.broadcasted_iota(jnp.int32, (nh, datt), 1)
        q8 = jnp.broadcast_to(q_ref[...], (nh, datt))
        qblk_ref[...] = jnp.where(col // hd == row, q8, 0.0).astype(BF16)
        m_ref[...] = jnp.full_like(m_ref, -jnp.inf)
        l_ref[...] = jnp.zeros_like(l_ref)
        acc_ref[...] = jnp.zeros_like(acc_ref)
        cf_ref[...] = jnp.zeros_like(cf_ref)

    qblk = qblk_ref[...]
    triu = (lax.broadcasted_iota(jnp.int32, (page, page), 0)
            <= lax.broadcasted_iota(jnp.int32, (page, page), 1)).astype(BF16)
    cf = cf_ref[...]
    scores = []
    for p in range(pg):
        s = jnp.dot(qblk, k_refs[p][...].astype(BF16), preferred_element_type=F32)
        f = _xdot_l(lf_refs[p][...], triu) + cf
        cf = f[:, page - 1:page]
        scores.append(s - f)
    cf_ref[...] = cf
    m_old = m_ref[...]
    m_new = m_old
    for s in scores:
        m_new = jnp.maximum(m_new, jnp.max(s, axis=1, keepdims=True))
    alpha = jnp.exp(m_old - m_new)
    l_new = alpha * l_ref[...]
    probs = []
    for s in scores:
        pr = jnp.exp(s - m_new)
        l_new = l_new + jnp.sum(pr, axis=1, keepdims=True)
        probs.append(pr)
    for h in range(nh):
        hr = slice(h * hd, (h + 1) * hd)
        a = acc_ref[hr, :] * alpha[h:h + 1, :]
        for p in range(pg):
            a = a + v_refs[p][hr, :] * probs[p][h:h + 1, :]
        acc_ref[hr, :] = a
    m_ref[...] = m_new
    l_ref[...] = l_new

    @pl.when(j == pl.num_programs(1) - 1)
    def _():
        kn = jnp.broadcast_to(kn_ref[...], (nh, datt)).astype(BF16).astype(F32)
        s_new = jnp.sum(qblk.astype(F32) * kn, axis=1, keepdims=True) - (cf + lfn_ref[...])
        m_fin = jnp.maximum(m_new, s_new)
        a_fin = jnp.exp(m_new - m_fin)
        p_new = jnp.exp(s_new - m_fin)
        l_fin = a_fin * l_new + p_new
        tot = jnp.sum(acc_ref[...], axis=1, keepdims=True)
        vcol = jnp.broadcast_to(vn_ref[...], (LANES, datt)).T[:, 0:1]
        out = (tot * head_rows(a_fin) + head_rows(p_new.astype(BF16).astype(F32)) * vcol) \
            / head_rows(l_fin)
        o_ref[...] = jnp.broadcast_to(out, (datt, LANES)).T[0:1, :]


def _attn_sample_call(page_table, q, kn, vn, lfn, ckt, cvt, clft, layer_base, *, nh, hd, pg):
    db, datt = q.shape
    n_pages = page_table.shape[1]
    page = ckt.shape[2]
    assert n_pages % pg == 0
    row3 = lambda a: a.reshape(db, 1, a.shape[1])
    rowspec = lambda w: pl.BlockSpec((None, 1, w), lambda b, j, pt: (b, 0, 0))

    def page_spec(p, rows):
        return pl.BlockSpec((None, rows, page), lambda b, j, pt: (layer_base + pt[b, j * pg + p], 0, 0))

    kern = functools.partial(_attn_sample_kernel, nh=nh, hd=hd, pg=pg)
    out = pl.pallas_call(
        kern,
        grid_spec=pltpu.PrefetchScalarGridSpec(
            num_scalar_prefetch=1,
            grid=(db, n_pages // pg),
            in_specs=([rowspec(datt), rowspec(datt), rowspec(datt),
                       pl.BlockSpec((None, nh, 1), lambda b, j, pt: (b, 0, 0))]
                      + [page_spec(p, datt) for p in range(pg)]
                      + [page_spec(p, datt) for p in range(pg)]
                      + [page_spec(p, nh) for p in range(pg)]),
            out_specs=rowspec(datt),
            scratch_shapes=[pltpu.VMEM((nh, datt), BF16), pltpu.VMEM((nh, 1), F32),
                            pltpu.VMEM((nh, 1), F32), pltpu.VMEM((datt, page), F32),
                            pltpu.VMEM((nh, 1), F32)]),
        out_shape=jax.ShapeDtypeStruct((db, 1, datt), F32),
        compiler_params=_cparams("parallel", "arbitrary"),
        name="attn_sample",
    )(page_table, row3(q), row3(kn), row3(vn), lfn.reshape(db, nh, 1),
      *([ckt] * pg), *([cvt] * pg), *([clft] * pg))
    return out.reshape(db, datt)


def _ssd_prompt_kernel(xc_ref, dt_ref, z_ref, alog_ref, dskip_ref, gn_ref, y_ref, hl_ref, st_ref,
                       *, nhs, p_dim, n_state, n_groups):
    @pl.when(pl.program_id(0) == 0)
    def _():
        st_ref[...] = jnp.zeros_like(st_ref)

    ln = xc_ref.shape[0]
    dssd = nhs * p_dim
    hpg = nhs // n_groups
    per = LANES // p_dim
    xs = xc_ref[:, 0:dssd]
    dt = dt_ref[...]
    a = -jnp.exp(alog_ref[...])
    expand = _expand_matrix(LANES, nhs, p_dim)
    acum = _xdot_r(_tri_matrix(ln), dt * a)
    acum_t = acum.T
    acum_x = _xdot_l(acum, expand)
    xdt = xs * _xdot_l(dt, expand)
    a_last_x = acum_x[ln - 1:ln, :]
    xdt_end = xdt * jnp.exp(a_last_x - acum_x)
    cdec = jnp.exp(acum_t[:, ln - 1:ln])
    causal = (lax.broadcasted_iota(jnp.int32, (ln, ln), 1)
              <= lax.broadcasted_iota(jnp.int32, (ln, ln), 0))
    lane = lax.broadcasted_iota(jnp.int32, (1, LANES), 1)

    y_parts = []
    for g in range(n_groups):
        bg = xc_ref[:, dssd + g * n_state:dssd + (g + 1) * n_state]
        cg = xc_ref[:, dssd + (n_groups + g) * n_state:dssd + (n_groups + g + 1) * n_state]
        gw = hpg * p_dim
        rows = slice(g * gw, (g + 1) * gw)
        st_g = st_ref[rows, :]
        cb = _dot_nt(cg, bg)
        y_off = _dot_nt(cg, st_g) * jnp.exp(acum_x[:, rows])
        for q in range(hpg // per):
            lsl = slice(g * gw + q * LANES, g * gw + (q + 1) * LANES)
            xdt_q = xdt[:, lsl].astype(BF16)
            y_q = jnp.zeros((ln, LANES), F32)
            for hh in range(per):
                h = g * hpg + q * per + hh
                seg = acum[:, h:h + 1] - acum_t[h:h + 1, :]
                m = cb * jnp.exp(jnp.where(causal, seg, -jnp.inf))
                yd = jnp.dot(m.astype(BF16), xdt_q, preferred_element_type=F32)
                y_q = jnp.where(lane // p_dim == hh, yd, y_q)
            y_parts.append(y_q + y_off[:, q * LANES:(q + 1) * LANES])
        cs = jnp.dot(xdt_end[:, rows].T.astype(BF16), bg.astype(BF16), preferred_element_type=F32)
        for hh in range(hpg):
            h = g * hpg + hh
            hr = slice(h * p_dim, (h + 1) * p_dim)
            dec = jnp.broadcast_to(cdec[h:h + 1, :], (p_dim, n_state))
            st_ref[hr, :] = st_ref[hr, :] * dec + cs[hh * p_dim:(hh + 1) * p_dim, :]
    y = jnp.concatenate(y_parts, axis=1)
    y_ref[...] = _gated_group_norm(y, xs, z_ref[...], dskip_ref[...], gn_ref[...], n_groups)
    hl_ref[...] = st_ref[...]


def _ssd_prompt_call(xc, dt, z, alog_pad, dskip_x, gnorm, *, nhs, p_dim, n_state, n_groups):
    t, dconv = xc.shape
    dssd = nhs * p_dim
    ln = SSD_CHUNK
    assert t % ln == 0 and LANES % p_dim == 0 and (nhs // n_groups) % (LANES // p_dim) == 0
    const = lambda i: (0, 0)
    kern = functools.partial(_ssd_prompt_kernel, nhs=nhs, p_dim=p_dim, n_state=n_state,
                             n_groups=n_groups)
    return pl.pallas_call(
        kern,
        grid=(t // ln,),
        in_specs=[pl.BlockSpec((ln, dconv), lambda i: (i, 0)),
                  pl.BlockSpec((ln, LANES), lambda i: (i, 0)),
                  pl.BlockSpec((ln, dssd), lambda i: (i, 0)),
                  pl.BlockSpec((1, LANES), const), pl.BlockSpec((1, dssd), const),
                  pl.BlockSpec((1, dssd), const)],
        out_specs=[pl.BlockSpec((ln, dssd), lambda i: (i, 0)),
                   pl.BlockSpec((dssd, n_state), const)],
        out_shape=[jax.ShapeDtypeStruct((t, dssd), BF16), jax.ShapeDtypeStruct((dssd, n_state), F32)],
        scratch_shapes=[pltpu.VMEM((dssd, n_state), F32)],
        compiler_params=_cparams("arbitrary"),
        name="ssd_prompt",
    )(xc, dt, z, alog_pad, dskip_x, gnorm)


def _ssd_sample_kernel(xc_ref, dt_ref, z_ref, alog_ref, dskip_ref, gn_ref, st_ref, y_ref, so_ref,
                       *, nhs, p_dim, n_state, n_groups):
    bb = xc_ref.shape[0]
    dssd = nhs * p_dim
    gw = dssd // n_groups
    xs = xc_ref[:, 0:dssd]
    dt = dt_ref[...]
    a = -jnp.exp(alog_ref[...])
    expand = _expand_matrix(LANES, nhs, p_dim)
    dec_x = _xdot_l(jnp.exp(dt * a), expand)
    xdt = xs * _xdot_l(dt, expand)

    def column(row):
        return jnp.broadcast_to(row, (n_state, dssd)).T

    ys = []
    for r in range(bb):
        bfull = jnp.concatenate(
            [jnp.broadcast_to(xc_ref[r:r + 1, dssd + g * n_state:dssd + (g + 1) * n_state],
                              (gw, n_state)) for g in range(n_groups)], axis=0)
        cfull = jnp.concatenate(
            [jnp.broadcast_to(xc_ref[r:r + 1, dssd + (n_groups + g) * n_state:
                                     dssd + (n_groups + g + 1) * n_state],
                              (gw, n_state)) for g in range(n_groups)], axis=0)
        s_new = st_ref[r] * column(dec_x[r:r + 1, :]) + column(xdt[r:r + 1, :]) * bfull
        so_ref[r] = s_new
        ycol = jnp.sum(s_new * cfull, axis=1, keepdims=True)
        ys.append(jnp.broadcast_to(ycol, (dssd, n_state)).T[0:1, :])
    y = jnp.concatenate(ys, axis=0)
    y_ref[...] = _gated_group_norm(y, xs, z_ref[...], dskip_ref[...], gn_ref[...], n_groups)


def _ssd_sample_call(xc, dt, z, alog_pad, dskip_x, gnorm, state, *, nhs, p_dim, n_state, n_groups):
    db, dconv = xc.shape
    dssd = nhs * p_dim
    bb = SUBLANES
    assert db % bb == 0 and n_state == LANES
    const = lambda i: (0, 0)
    kern = functools.partial(_ssd_sample_kernel, nhs=nhs, p_dim=p_dim, n_state=n_state,
                             n_groups=n_groups)
    return pl.pallas_call(
        kern,
        grid=(db // bb,),
        in_specs=[pl.BlockSpec((bb, dconv), lambda i: (i, 0)),
                  pl.BlockSpec((bb, LANES), lambda i: (i, 0)),
                  pl.BlockSpec((bb, dssd), lambda i: (i, 0)),
                  pl.BlockSpec((1, LANES), const), pl.BlockSpec((1, dssd), const),
                  pl.BlockSpec((1, dssd), const),
                  pl.BlockSpec((bb, dssd, n_state), lambda i: (i, 0, 0))],
        out_specs=[pl.BlockSpec((bb, dssd), lambda i: (i, 0)),
                   pl.BlockSpec((bb, dssd, n_state), lambda i: (i, 0, 0))],
        out_shape=[jax.ShapeDtypeStruct((db, dssd), BF16),
                   jax.ShapeDtypeStruct((db, dssd, n_state), F32)],
        compiler_params=_cparams("parallel"),
        name="ssd_sample",
    )(xc, dt, z, alog_pad, dskip_x, gnorm, state)


def _pad_lanes(v):
    return jnp.zeros((1, LANES), F32).at[0, :v.shape[0]].set(v)


def kernel(x_prompt, x_sample, cache_k, cache_v, cache_logf, state_ssm, state_conv, page_table, c_prompt, c_sample, w_ada, b_ada, norm_ffn1, w_ffn1_in, w_ffn1_out, norm_mix, w_in, b_forget, conv_w, conv_b, dt_bias, a_log, d_skip, ssd_norm, w_out, norm_ffn2, w_ffn2_in, w_ffn2_out, norm_final):
    bsz, t, d = x_prompt.shape
    db, s_new, _ = x_sample.shape
    depth, n_phys, page, nh, hd = cache_k.shape
    _, _, nhs, p_dim, n_state = state_ssm.shape
    conv_w1, dconv = state_conv.shape[2:]
    datt, dssd = nh * hd, nhs * p_dim
    n_groups = (dconv - dssd) // (2 * n_state)
    assert bsz == 1 and s_new == 1 and db % SUBLANES == 0 and c_prompt.shape[0] == 1
    scale = hd ** -0.5
    prompt_row_block = db // SUBLANES
    tm = min(512, t)
    n_pages = page_table.shape[1]
    pg = min(8, n_pages)

    xp = x_prompt.reshape(t, d)
    xs_ = x_sample.reshape(db, d)
    ckt = jnp.transpose(cache_k, (0, 1, 3, 4, 2)).reshape(depth * n_phys, datt, page)
    cvt = jnp.transpose(cache_v, (0, 1, 3, 4, 2)).reshape(depth * n_phys, datt, page)
    clft = jnp.transpose(cache_logf, (0, 1, 3, 2)).reshape(depth * n_phys, nh, page)
    heads_last = lambda at, n: jnp.transpose(at.reshape(nh, hd, n), (2, 0, 1))
    c_all = jnp.concatenate([c_sample, c_prompt, jnp.zeros((SUBLANES - 1, d), F32)], axis=0)

    outs_p, outs_s = [], []
    for l in range(depth):
        sp = (datt, 2 * datt, 3 * datt, 3 * datt + nh, 3 * datt + nh + dssd,
              3 * datt + nh + dssd + dconv)
        wi = w_in[l]
        wqkv = wi[:, :sp[2]].astype(BF16)
        wzx = wi[:, sp[3]:sp[5]].astype(BF16)
        wfd = (jnp.zeros((d, 2 * LANES), F32).at[:, :nh].set(wi[:, sp[2]:sp[3]])
               .at[:, LANES:LANES + nhs].set(wi[:, sp[5]:])).astype(BF16)
        bfd = jnp.concatenate([_pad_lanes(b_forget[l]), _pad_lanes(dt_bias[l])], axis=1)
        alog_pad = _pad_lanes(a_log[l])
        dskip_x = jnp.repeat(d_skip[l], p_dim).reshape(1, dssd)
        gnorm = ssd_norm[l].reshape(1, dssd)
        w1i, w1o = w_ffn1_in[l].astype(BF16), w_ffn1_out[l].astype(BF16)
        w2i, w2o = w_ffn2_in[l].astype(BF16), w_ffn2_out[l].astype(BF16)
        wo = w_out[l].astype(BF16)
        last = l == depth - 1

        m_all = _ada_call(c_all, w_ada[l], b_ada[l])
        proj = dict(datt=datt, dssd=dssd, nh=nh, nhs=nhs, scale=scale)
        ssd = dict(nhs=nhs, p_dim=p_dim, n_state=n_state, n_groups=n_groups)

        xp = _ffn_call(xp, m_all, 0, norm_ffn1[l], w1i, w1o, per_row=False, tm=tm,
                       prompt_row_block=prompt_row_block, name="ffn1_prompt")
        (q, kt, vt, kb, vb, lft, fcol, frow, z, xc, dt, tail) = _inproj_prompt_call(
            xp, m_all, prompt_row_block, norm_mix[l], wqkv, wzx, wfd, bfd, conv_w[l], conv_b[l],
            tm=tm, **proj)
        att = _attn_prompt_call(q, kb, vb, fcol, frow, nh=nh, hd=hd, tq=tm)
        yssd, h_last = _ssd_prompt_call(xc, dt, z, alog_pad, dskip_x, gnorm, **ssd)
        xp = _ffn_call(xp, m_all, 6, norm_ffn2[l], w2i, w2o, per_row=False, tm=tm,
                       prompt_row_block=prompt_row_block, mix=(att, yssd, wo, 5),
                       final_norm_w=norm_final if last else None, name="ffn2_prompt")
        outs_p.append((heads_last(kt, t)[None], heads_last(vt, t)[None], lft.T[None],
                       h_last.reshape(1, nhs, p_dim, n_state),
                       tail[SUBLANES - conv_w1:].reshape(1, conv_w1, dconv)))

        xs_ = _ffn_call(xs_, m_all, 0, norm_ffn1[l], w1i, w1o, per_row=True, tm=db,
                        prompt_row_block=prompt_row_block, name="ffn1_sample")
        (q, k, v, kt, vt, logf, lft, z, xc, dt, conv_new) = _inproj_sample_call(
            xs_, m_all, norm_mix[l], wqkv, wzx, wfd, bfd, conv_w[l], conv_b[l],
            jnp.transpose(state_conv[l], (1, 0, 2)), **proj)
        att = _attn_sample_call(page_table, q, k, v, logf, ckt, cvt, clft, l * n_phys,
                                nh=nh, hd=hd, pg=pg)
        yssd, st_new = _ssd_sample_call(xc, dt, z, alog_pad, dskip_x, gnorm,
                                        state_ssm[l].reshape(db, dssd, n_state), **ssd)
        xs_ = _ffn_call(xs_, m_all, 6, norm_ffn2[l], w2i, w2o, per_row=True, tm=db,
                        prompt_row_block=prompt_row_block, mix=(att, yssd, wo, 5),
                        final_norm_w=norm_final if last else None, name="ffn2_sample")
        outs_s.append((heads_last(kt, db)[:, None], heads_last(vt, db)[:, None], lft.T[:, None],
                       st_new.reshape(db, nhs, p_dim, n_state), jnp.transpose(conv_new, (1, 0, 2))))

    stack = lambda outs, i: jnp.stack([o[i] for o in outs])
    return (xp.reshape(bsz, t, d), xs_.reshape(db, 1, d),
            stack(outs_p, 0), stack(outs_p, 1), stack(outs_p, 2), stack(outs_p, 3), stack(outs_p, 4),
            stack(outs_s, 0), stack(outs_s, 1), stack(outs_s, 2), stack(outs_s, 3), stack(outs_s, 4))
```

```python
import functools

import math

import jax
import jax.numpy as jnp
import numpy as np
from jax import lax
from jax.experimental import pallas as pl
from jax.experimental.pallas import tpu as pltpu

F32 = jnp.float32
BF16 = jnp.bfloat16
EPS = 1e-6
LOG2E = math.log2(math.e)
LANES = 128
SUBLANES = 8
SSD_CHUNK = 128
VMEM_LIMIT = 56 * 1024 * 1024


def _cparams(*semantics):
    return pltpu.CompilerParams(dimension_semantics=semantics, vmem_limit_bytes=VMEM_LIMIT)


def _dot(a, b):
    return jnp.dot(a.astype(BF16), b.astype(BF16), preferred_element_type=F32)


def _dot_nt(a, b):
    return lax.dot_general(a.astype(BF16), b.astype(BF16), (((1,), (1,)), ((), ())),
                           preferred_element_type=F32)


def _split3(a):
    hi = a.astype(BF16)
    r = a - hi.astype(F32)
    mid = r.astype(BF16)
    lo = (r - mid.astype(F32)).astype(BF16)
    return hi, mid, lo


def _xdot_l(a, e):
    hi, mid, lo = _split3(a)
    f = functools.partial(jnp.dot, preferred_element_type=F32)
    return f(hi, e) + f(mid, e) + f(lo, e)


def _xdot_r(e, b):
    hi, mid, lo = _split3(b)
    f = functools.partial(jnp.dot, preferred_element_type=F32)
    return f(e, hi) + f(e, mid) + f(e, lo)


def _silu(x):
    return x * jax.nn.sigmoid(x)


def _softplus(x):
    return jnp.maximum(x, 0.0) + jnp.log1p(jnp.exp(-jnp.abs(x)))


def _rms(x):
    return x * lax.rsqrt(jnp.mean(x * x, axis=-1, keepdims=True) + EPS)


def _norm_mod(x, nw, shift, scale):
    return (_rms(x) * nw) * (1.0 + scale) + shift


def _expand_matrix(n_rows, n_heads, width):
    row = lax.broadcasted_iota(jnp.int32, (n_rows, n_heads * width), 0)
    col = lax.broadcasted_iota(jnp.int32, (n_rows, n_heads * width), 1)
    return (col // width == row).astype(BF16)


def _tri_matrix(n):
    row = lax.broadcasted_iota(jnp.int32, (n, n), 0)
    col = lax.broadcasted_iota(jnp.int32, (n, n), 1)
    return (col <= row).astype(BF16)


def _gated_group_norm(y, xs, z, dskip, gnorm, n_groups):
    y = (y + dskip * xs) * _silu(z)
    gw = y.shape[1] // n_groups
    parts = [_rms(y[:, g * gw:(g + 1) * gw]) for g in range(n_groups)]
    return (jnp.concatenate(parts, axis=1) * gnorm).astype(BF16)


def _ada_kernel(c_ref, w_ref, b_ref, o_ref):
    o_ref[...] = _dot(_silu(c_ref[...]), w_ref[...]) + b_ref[...]


def _ada_call(c_all, w, b):
    r, d = c_all.shape
    n = w.shape[1]
    tn = d
    return pl.pallas_call(
        _ada_kernel,
        grid=(n // tn,),
        in_specs=[pl.BlockSpec((r, d), lambda j: (0, 0)),
                  pl.BlockSpec((d, tn), lambda j: (0, j)),
                  pl.BlockSpec((1, tn), lambda j: (0, j))],
        out_specs=pl.BlockSpec((r, tn), lambda j: (0, j)),
        out_shape=jax.ShapeDtypeStruct((r, n), F32),
        compiler_params=_cparams("parallel"),
        name="ada",
    )(c_all, w, b.reshape(1, n))


def _mod_spec(per_row, tm, d, idx, prompt_row_block):
    if per_row:
        return pl.BlockSpec((tm, d), lambda i: (i, idx))
    return pl.BlockSpec((SUBLANES, d), lambda i: (prompt_row_block, idx))


def _ffn_kernel(*refs, per_row, with_mix, final_norm, ff, tf, datt):
    it = iter(refs)
    x_ref = next(it)
    if with_mix:
        att_ref, ys_ref, wo_ref, g2_ref = next(it), next(it), next(it), next(it)
    sh_ref, sc_ref, g_ref, nw_ref, win_ref, wout_ref = (next(it) for _ in range(6))
    nf_ref = next(it) if final_norm else None
    o_ref = next(it)
    a_ref = next(it)
    rows = slice(None) if per_row else slice(0, 1)

    x = x_ref[...]
    if with_mix:
        mix = (jnp.dot(att_ref[...].astype(BF16), wo_ref[0:datt, :], preferred_element_type=F32)
               + jnp.dot(ys_ref[...], wo_ref[datt:, :], preferred_element_type=F32))
        x = x + g2_ref[rows, :] * mix
    h = _norm_mod(x, nw_ref[...], sh_ref[rows, :], sc_ref[rows, :]).astype(BF16)
    for c in range(ff // tf):
        g = jnp.dot(h, win_ref[:, c * tf:(c + 1) * tf], preferred_element_type=F32)
        u = jnp.dot(h, win_ref[:, ff + c * tf:ff + (c + 1) * tf], preferred_element_type=F32)
        a_ref[:, c * tf:(c + 1) * tf] = (_silu(g) * u).astype(BF16)
    y = x + (0.5 * g_ref[rows, :]) * jnp.dot(a_ref[...], wout_ref[...], preferred_element_type=F32)
    if final_norm:
        y = _rms(y) * nf_ref[...]
    o_ref[...] = y


def _ffn_call(x, m_all, ada_base, norm_w, w_in_b, w_out_b, *, per_row, tm, prompt_row_block,
              mix=None, final_norm_w=None, name):
    r, d = x.shape
    ff = w_out_b.shape[0]
    tf = 256
    assert r % tm == 0 and ff % tf == 0
    const = lambda i: (0, 0)
    resident = functools.partial(pl.BlockSpec, index_map=const, pipeline_mode=pl.Buffered(1))
    mspec = functools.partial(_mod_spec, per_row, tm, d, prompt_row_block=prompt_row_block)
    args, specs = [x], [pl.BlockSpec((tm, d), lambda i: (i, 0))]
    datt = 0
    if mix is not None:
        att, ys, wo_b, g2_idx = mix
        datt = att.shape[1]
        args += [att, ys, wo_b, m_all]
        specs += [pl.BlockSpec((tm, datt), lambda i: (i, 0)),
                  pl.BlockSpec((tm, ys.shape[1]), lambda i: (i, 0)),
                  resident(wo_b.shape), mspec(g2_idx)]
    args += [m_all, m_all, m_all, norm_w.reshape(1, d), w_in_b, w_out_b]
    specs += [mspec(ada_base), mspec(ada_base + 1), mspec(ada_base + 2),
              pl.BlockSpec((1, d), const), resident(w_in_b.shape), resident(w_out_b.shape)]
    if final_norm_w is not None:
        args.append(final_norm_w.reshape(1, d))
        specs.append(pl.BlockSpec((1, d), const))
    kern = functools.partial(_ffn_kernel, per_row=per_row, with_mix=mix is not None,
                             final_norm=final_norm_w is not None, ff=ff, tf=tf, datt=datt)
    return pl.pallas_call(
        kern,
        grid=(r // tm,),
        in_specs=specs,
        out_specs=pl.BlockSpec((tm, d), lambda i: (i, 0)),
        out_shape=jax.ShapeDtypeStruct((r, d), F32),
        scratch_shapes=[pltpu.VMEM((tm, ff), BF16)],
        compiler_params=_cparams("parallel"),
        name=name,
    )(*args)


def _project(x, sh, sc, nw_ref, wqkv_ref, wzx_ref, wfd_ref, bfd_ref, *, datt, dssd, nh, nhs):
    h = _norm_mod(x, nw_ref[...], sh, sc).astype(BF16)
    qkv = jnp.dot(h, wqkv_ref[...], preferred_element_type=F32)
    zx = jnp.dot(h, wzx_ref[...], preferred_element_type=F32)
    fd = jnp.dot(h, wfd_ref[...], preferred_element_type=F32) + bfd_ref[...]
    lane = lax.broadcasted_iota(jnp.int32, (1, LANES), 1)
    logf = jnp.where(lane < nh, -_softplus(-fd[:, :LANES]), 0.0)
    dt = jnp.where(lane < nhs, _softplus(fd[:, LANES:]), 0.0)
    return (qkv[:, :datt], qkv[:, datt:2 * datt], qkv[:, 2 * datt:], zx[:, :dssd], zx[:, dssd:],
            logf, dt)


def _inproj_prompt_kernel(x_ref, sh_ref, sc_ref, nw_ref, wqkv_ref, wzx_ref, wfd_ref, bfd_ref,
                          cw_ref, cb_ref, selq_ref, qone_ref, selk_ref, kone_ref,
                          qa_ref, ka_ref, vbt_ref, kt_ref, vt_ref, lft_ref, z_ref,
                          xc_ref, dt_ref, tail_ref,
                          ptail_ref, cf_ref, *, datt, dssd, nh, nhs, hd, scale, conv_w):
    @pl.when(pl.program_id(0) == 0)
    def _():
        ptail_ref[...] = jnp.zeros_like(ptail_ref)
        cf_ref[...] = jnp.zeros_like(cf_ref)

    tm = x_ref.shape[0]
    q, k, v, z, xbc, logf, dt = _project(
        x_ref[...], sh_ref[0:1, :], sc_ref[0:1, :], nw_ref, wqkv_ref, wzx_ref, wfd_ref, bfd_ref,
        datt=datt, dssd=dssd, nh=nh, nhs=nhs)
    kt_ref[...] = k.T
    vt = v.T
    vt_ref[...] = vt
    vbt_ref[...] = vt.astype(BF16)
    z_ref[...] = z
    lft_ref[...] = logf.T[0:SUBLANES, :]
    dt_ref[...] = dt

    fc = _xdot_r(_tri_matrix(tm), logf) + cf_ref[...]
    cf_ref[...] = fc[tm - 1:tm, :]

    per = LANES // hd
    lane = lax.broadcasted_iota(jnp.int32, (1, LANES), 1)
    f3 = jnp.concatenate(_split3(fc * LOG2E), axis=1)
    qs = q * (scale * LOG2E)
    for h in range(nh):
        g, hh = divmod(h, per)
        qpair = qs[:, g * LANES:(g + 1) * LANES]
        qa_ref[:, 2 * h * LANES:(2 * h + 1) * LANES] = jnp.where(lane // hd == hh, qpair, 0.0).astype(BF16)
        qbias = jnp.dot(f3, selq_ref[h], preferred_element_type=F32) + qone_ref[h:h + 1, :]
        qa_ref[:, (2 * h + 1) * LANES:(2 * h + 2) * LANES] = qbias.astype(BF16)
    for g in range(nh // per):
        ka_ref[:, 2 * g * LANES:(2 * g + 1) * LANES] = k[:, g * LANES:(g + 1) * LANES].astype(BF16)
        kbias = jnp.dot(f3, selk_ref[g], preferred_element_type=F32) + kone_ref[...]
        ka_ref[:, (2 * g + 1) * LANES:(2 * g + 2) * LANES] = kbias.astype(BF16)

    ptail = ptail_ref[...]
    row8 = lax.broadcasted_iota(jnp.int32, ptail.shape, 0)
    acc = cb_ref[...]
    for i in range(conv_w):
        s = conv_w - 1 - i
        if s == 0:
            xs = xbc
        else:
            r = pltpu.roll(xbc, s, 0)
            top = jnp.where(row8 < s, pltpu.roll(ptail, s, 0), r[0:SUBLANES])
            xs = jnp.concatenate([top, r[SUBLANES:]], axis=0)
        acc = acc + xs * cw_ref[i:i + 1, :]
    xc_ref[...] = _silu(acc)
    tail = xbc[tm - SUBLANES:tm]
    ptail_ref[...] = tail
    tail_ref[...] = tail


def _bias_selectors(nh, hd):
    per = LANES // hd
    selq = np.zeros((nh, 3 * LANES, LANES), np.float32)
    qone = np.zeros((nh, LANES), np.float32)
    selk = np.zeros((nh // per, 3 * LANES, LANES), np.float32)
    kone = np.zeros((1, LANES), np.float32)
    kone[0, 0:3] = 1.0
    for h in range(nh):
        g, hh = divmod(h, per)
        for c in range(3):
            selq[h, c * LANES + h, c] = 1.0
            selk[g, c * LANES + h, 3 + 3 * hh + c] = -1.0
        qone[h, 3 + 3 * hh:6 + 3 * hh] = 1.0
    return (jnp.asarray(selq, BF16), jnp.asarray(qone, F32), jnp.asarray(selk, BF16),
            jnp.asarray(kone, F32))


def _inproj_prompt_call(x, m_all, prompt_row_block, norm_w, wqkv, wzx, wfd, bfd, cw, cb, *,
                        datt, dssd, nh, nhs, hd, scale, tm):
    t, d = x.shape
    dconv = cw.shape[1]
    per = LANES // hd
    assert t % tm == 0 and nh == SUBLANES and conv_w_ok(cw.shape[0]) and 3 + 3 * per <= LANES
    const = lambda i: (0, 0)
    full = lambda a: pl.BlockSpec(a.shape, lambda i: (0,) * a.ndim)
    rowblk = lambda w: pl.BlockSpec((tm, w), lambda i: (i, 0))
    colblk = lambda h: pl.BlockSpec((h, tm), lambda i: (0, i))
    mspec = functools.partial(_mod_spec, False, tm, d, prompt_row_block=prompt_row_block)
    kern = functools.partial(_inproj_prompt_kernel, datt=datt, dssd=dssd, nh=nh, nhs=nhs, hd=hd,
                             scale=scale, conv_w=cw.shape[0])
    sel = _bias_selectors(nh, hd)
    sds = jax.ShapeDtypeStruct
    return pl.pallas_call(
        kern,
        grid=(t // tm,),
        in_specs=[rowblk(d), mspec(3), mspec(4), pl.BlockSpec((1, d), const),
                  full(wqkv), full(wzx), full(wfd), full(bfd), full(cw),
                  pl.BlockSpec((1, dconv), const)] + [full(a) for a in sel],
        out_specs=[rowblk(2 * nh * LANES), rowblk(2 * (nh // per) * LANES), colblk(datt),
                   colblk(datt), colblk(datt), colblk(SUBLANES),
                   rowblk(dssd), rowblk(dconv), rowblk(LANES),
                   pl.BlockSpec((SUBLANES, dconv), const)],
        out_shape=[sds((t, 2 * nh * LANES), BF16), sds((t, 2 * (nh // per) * LANES), BF16),
                   sds((datt, t), BF16), sds((datt, t), F32), sds((datt, t), F32),
                   sds((SUBLANES, t), F32), sds((t, dssd), F32),
                   sds((t, dconv), F32), sds((t, LANES), F32), sds((SUBLANES, dconv), F32)],
        scratch_shapes=[pltpu.VMEM((SUBLANES, dconv), F32), pltpu.VMEM((1, LANES), F32)],
        compiler_params=_cparams("arbitrary"),
        name="inproj_prompt",
    )(x, m_all, m_all, norm_w.reshape(1, d), wqkv, wzx, wfd, bfd, cw, cb.reshape(1, dconv), *sel)


def conv_w_ok(w):
    return 1 <= w - 1 < SUBLANES


def _inproj_sample_kernel(x_ref, sh_ref, sc_ref, nw_ref, wqkv_ref, wzx_ref, wfd_ref, bfd_ref,
                          cw_ref, cb_ref, cs_ref,
                          q_ref, k_ref, v_ref, kt_ref, vt_ref, lf_ref, lft_ref, z_ref, xc_ref, dt_ref,
                          cn_ref, *, datt, dssd, nh, nhs, scale, conv_w):
    q, k, v, z, xbc, logf, dt = _project(
        x_ref[...], sh_ref[...], sc_ref[...], nw_ref, wqkv_ref, wzx_ref, wfd_ref, bfd_ref,
        datt=datt, dssd=dssd, nh=nh, nhs=nhs)
    q_ref[...] = q * scale
    k_ref[...] = k
    v_ref[...] = v
    kt_ref[...] = k.T
    vt_ref[...] = v.T
    z_ref[...] = z
    lf_ref[...] = logf[:, 0:nh]
    lft_ref[...] = logf.T[0:SUBLANES, :]
    dt_ref[...] = dt
    acc = cb_ref[...]
    for i in range(conv_w - 1):
        acc = acc + cs_ref[i] * cw_ref[i:i + 1, :]
    acc = acc + xbc * cw_ref[conv_w - 1:conv_w, :]
    xc_ref[...] = _silu(acc)
    for i in range(conv_w - 2):
        cn_ref[i] = cs_ref[i + 1]
    cn_ref[conv_w - 2] = xbc


def _inproj_sample_call(x, m_all, norm_w, wqkv, wzx, wfd, bfd, cw, cb, conv_state, *,
                        datt, dssd, nh, nhs, scale):
    r, d = x.shape
    conv_w, dconv = cw.shape
    const = lambda i: (0, 0)
    full = lambda a: pl.BlockSpec(a.shape, lambda i: (0,) * a.ndim)
    mspec = functools.partial(_mod_spec, True, r, d, prompt_row_block=0)
    kern = functools.partial(_inproj_sample_kernel, datt=datt, dssd=dssd, nh=nh, nhs=nhs,
                             scale=scale, conv_w=conv_w)
    sds = jax.ShapeDtypeStruct
    shapes = [sds((r, datt), F32), sds((r, datt), F32), sds((r, datt), F32), sds((datt, r), F32),
              sds((datt, r), F32), sds((r, nh), F32), sds((SUBLANES, r), F32), sds((r, dssd), F32),
              sds((r, dconv), F32), sds((r, LANES), F32), sds(conv_state.shape, F32)]
    return pl.pallas_call(
        kern,
        grid=(1,),
        in_specs=[full(x), mspec(3), mspec(4), pl.BlockSpec((1, d), const), full(wqkv), full(wzx),
                  full(wfd), full(bfd), full(cw), pl.BlockSpec((1, dconv), const), full(conv_state)],
        out_specs=[full(s) for s in shapes],
        out_shape=shapes,
        compiler_params=_cparams("arbitrary"),
        name="inproj_sample",
    )(x, m_all, m_all, norm_w.reshape(1, d), wqkv, wzx, wfd, bfd, cw, cb.reshape(1, dconv), conv_state)


def _attn_prompt_kernel(qi_ref, kj_ref, q_ref, k_ref, vt_ref, o_ref, m_ref, l_ref, acc_ref, *, nh, hd):
    step_id = pl.program_id(0)
    i = qi_ref[step_id]
    j = kj_ref[step_id]
    tq, tk = q_ref.shape[0], k_ref.shape[0]
    per = LANES // hd
    gw = 2 * LANES

    @pl.when(j == 0)
    def _():
        m_ref[...] = jnp.full_like(m_ref, -jnp.inf)
        l_ref[...] = jnp.zeros_like(l_ref)
        acc_ref[...] = jnp.zeros_like(acc_ref)

    ones = jnp.ones((2 * SUBLANES, tk), BF16)

    def step(masked):
        if masked:
            keep = (lax.broadcasted_iota(jnp.int32, (tk, tq), 0)
                    <= lax.broadcasted_iota(jnp.int32, (tk, tq), 1))
        def scores(h):
            g = h // per
            return lax.dot_general(k_ref[:, g * gw:(g + 1) * gw], q_ref[:, h * gw:(h + 1) * gw],
                                   (((1,), (1,)), ((), ())), preferred_element_type=F32)

        def softmax(h, st):
            if masked:
                st = jnp.where(keep, st, -jnp.inf)
            m_old = m_ref[h:h + 1, :]
            m_new = jnp.maximum(m_old, jnp.max(st, axis=0, keepdims=True))
            m_ref[h:h + 1, :] = m_new
            return jnp.exp2(st - m_new).astype(BF16), jnp.exp2(m_old - m_new)

        def values(h, p, alpha):
            hr = slice(h * hd, (h + 1) * hd)
            pv = jnp.dot(jnp.concatenate([vt_ref[hr, :], ones], axis=0), p,
                         preferred_element_type=F32)
            acc_ref[hr, :] = alpha * acc_ref[hr, :] + pv[0:hd]
            l_ref[h:h + 1, :] = alpha * l_ref[h:h + 1, :] + pv[hd:hd + 1]

        st = {0: scores(0)}
        pa = {}
        for h in range(nh + 1):
            if h + 1 < nh:
                st[h + 1] = scores(h + 1)
            if h < nh:
                pa[h] = softmax(h, st.pop(h))
            if h >= 1:
                values(h - 1, *pa.pop(h - 1))

    @pl.when(j < i)
    def _():
        step(False)

    @pl.when(j == i)
    def _():
        step(True)
        out_t = jnp.concatenate(
            [acc_ref[h * hd:(h + 1) * hd, :] * (1.0 / l_ref[h:h + 1, :]) for h in range(nh)], axis=0)
        o_ref[...] = out_t.T.astype(BF16)


def _attn_prompt_call(qa, ka, vbt, *, nh, hd, tq):
    t = qa.shape[0]
    datt = nh * hd
    per = LANES // hd
    assert t % tq == 0 and LANES % hd == 0 and nh % per == 0
    nq = t // tq
    qi = np.concatenate([np.full(i + 1, i, np.int32) for i in range(nq)])
    kj = np.concatenate([np.arange(i + 1, dtype=np.int32) for i in range(nq)])
    kern = functools.partial(_attn_prompt_kernel, nh=nh, hd=hd)
    return pl.pallas_call(
        kern,
        grid_spec=pltpu.PrefetchScalarGridSpec(
            num_scalar_prefetch=2,
            grid=(len(qi),),
            in_specs=[pl.BlockSpec((tq, qa.shape[1]), lambda s, qi, kj: (qi[s], 0)),
                      pl.BlockSpec((tq, ka.shape[1]), lambda s, qi, kj: (kj[s], 0)),
                      pl.BlockSpec((datt, tq), lambda s, qi, kj: (0, kj[s]))],
            out_specs=pl.BlockSpec((tq, datt), lambda s, qi, kj: (qi[s], 0)),
            scratch_shapes=[pltpu.VMEM((nh, tq), F32), pltpu.VMEM((nh, tq), F32),
                            pltpu.VMEM((datt, tq), F32)]),
        out_shape=jax.ShapeDtypeStruct((t, datt), BF16),
        compiler_params=_cparams("arbitrary"),
        name="attn_prompt",
    )(jnp.asarray(qi), jnp.asarray(kj), qa, ka, vbt)


def _attn_sample_kernel(pt_ref, q_ref, kn_ref, vn_ref, lfn_ref, *rest, nh, hd, pg):
    k_refs, v_refs, lf_refs = rest[0:pg], rest[pg:2 * pg], rest[2 * pg:3 * pg]
    o_ref, qblk_ref, m_ref, l_ref, acc_ref, cf_ref = rest[3 * pg:]
    j = pl.program_id(1)
    datt, page = k_refs[0].shape

    def head_rows(a):
        return jnp.concatenate([jnp.broadcast_to(a[h:h + 1, :], (hd, a.shape[1])) for h in range(nh)],
                               axis=0)

    @pl.when(j == 0)
    def _():
        row = lax.broadcasted_iota(jnp.int32, (nh, datt), 0)
        col = lax.broadcasted_iota(jnp.int32, (nh, datt), 1)
        q8 = jnp.broadcast_to(q_ref[...], (nh, datt))
        qblk_ref[...] = jnp.where(col // hd == row, q8, 0.0).astype(BF16)
        m_ref[...] = jnp.full_like(m_ref, -jnp.inf)
        l_ref[...] = jnp.zeros_like(l_ref)
        acc_ref[...] = jnp.zeros_like(acc_ref)
        cf_ref[...] = jnp.zeros_like(cf_ref)

    qblk = qblk_ref[...]
    triu = (lax.broadcasted_iota(jnp.int32, (page, page), 0)
            <= lax.broadcasted_iota(jnp.int32, (page, page), 1)).astype(BF16)
    cf = cf_ref[...]
    f_loc = _xdot_l(jnp.concatenate([r[...] for r in lf_refs], axis=0), triu)
    scores = []
    for p in range(pg):
        s = jnp.dot(qblk, k_refs[p][...].astype(BF16), preferred_element_type=F32)
        f = f_loc[p * nh:(p + 1) * nh, :] + cf
        cf = f[:, page - 1:page]
        scores.append(s - f)
    cf_ref[...] = cf
    m_old = m_ref[...]
    m_new = m_old
    for s in scores:
        m_new = jnp.maximum(m_new, jnp.max(s, axis=1, keepdims=True))
    alpha = jnp.exp(m_old - m_new)
    l_new = alpha * l_ref[...]
    probs = []
    for s in scores:
        pr = jnp.exp(s - m_new)
        l_new = l_new + jnp.sum(pr, axis=1, keepdims=True)
        probs.append(pr)
    for h in range(nh):
        hr = slice(h * hd, (h + 1) * hd)
        a = acc_ref[hr, :] * alpha[h:h + 1, :]
        for p in range(pg):
            a = a + v_refs[p][hr, :] * probs[p][h:h + 1, :]
        acc_ref[hr, :] = a
    m_ref[...] = m_new
    l_ref[...] = l_new

    @pl.when(j == pl.num_programs(1) - 1)
    def _():
        kn = jnp.broadcast_to(kn_ref[...], (nh, datt)).astype(BF16).astype(F32)
        s_new = jnp.sum(qblk.astype(F32) * kn, axis=1, keepdims=True) - (cf + lfn_ref[...])
        m_fin = jnp.maximum(m_new, s_new)
        a_fin = jnp.exp(m_new - m_fin)
        p_new = jnp.exp(s_new - m_fin)
        l_fin = a_fin * l_new + p_new
        tot = jnp.sum(acc_ref[...], axis=1, keepdims=True)
        vcol = jnp.broadcast_to(vn_ref[...], (LANES, datt)).T[:, 0:1]
        out = (tot * head_rows(a_fin) + head_rows(p_new.astype(BF16).astype(F32)) * vcol) \
            / head_rows(l_fin)
        o_ref[...] = jnp.broadcast_to(out, (datt, LANES)).T[0:1, :]


def _attn_sample_call(page_table, q, kn, vn, lfn, ckt, cvt, clft, layer_base, *, nh, hd, pg):
    db, datt = q.shape
    n_pages = page_table.shape[1]
    page = ckt.shape[2]
    assert n_pages % pg == 0
    row3 = lambda a: a.reshape(db, 1, a.shape[1])
    rowspec = lambda w: pl.BlockSpec((None, 1, w), lambda b, j, pt: (b, 0, 0))

    def page_spec(p, rows):
        return pl.BlockSpec((None, rows, page), lambda b, j, pt: (layer_base + pt[b, j * pg + p], 0, 0))

    kern = functools.partial(_attn_sample_kernel, nh=nh, hd=hd, pg=pg)
    out = pl.pallas_call(
        kern,
        grid_spec=pltpu.PrefetchScalarGridSpec(
            num_scalar_prefetch=1,
            grid=(db, n_pages // pg),
            in_specs=([rowspec(datt), rowspec(datt), rowspec(datt),
                       pl.BlockSpec((None, nh, 1), lambda b, j, pt: (b, 0, 0))]
                      + [page_spec(p, datt) for p in range(pg)]
                      + [page_spec(p, datt) for p in range(pg)]
                      + [page_spec(p, nh) for p in range(pg)]),
            out_specs=rowspec(datt),
            scratch_shapes=[pltpu.VMEM((nh, datt), BF16), pltpu.VMEM((nh, 1), F32),
                            pltpu.VMEM((nh, 1), F32), pltpu.VMEM((datt, page), F32),
                            pltpu.VMEM((nh, 1), F32)]),
        out_shape=jax.ShapeDtypeStruct((db, 1, datt), F32),
        compiler_params=_cparams("parallel", "arbitrary"),
        name="attn_sample",
    )(page_table, row3(q), row3(kn), row3(vn), lfn.reshape(db, nh, 1),
      *([ckt] * pg), *([cvt] * pg), *([clft] * pg))
    return out.reshape(db, datt)


def _ssd_prompt_kernel(xc_ref, dt_ref, z_ref, alog_ref, dskip_ref, gn_ref, y_ref, hl_ref, st_ref,
                       *, nhs, p_dim, n_state, n_groups):
    @pl.when(pl.program_id(0) == 0)
    def _():
        st_ref[...] = jnp.zeros_like(st_ref)

    ln = xc_ref.shape[0]
    dssd = nhs * p_dim
    hpg = nhs // n_groups
    per = LANES // p_dim
    xs = xc_ref[:, 0:dssd]
    dt = dt_ref[...]
    a = -jnp.exp(alog_ref[...])
    expand = _expand_matrix(LANES, nhs, p_dim)
    acum = _xdot_r(_tri_matrix(ln), dt * a)
    acum_t = acum.T
    acum_x = _xdot_l(acum, expand)
    xdt = xs * _xdot_l(dt, expand)
    a_last_x = acum_x[ln - 1:ln, :]
    xdt_end = xdt * jnp.exp(a_last_x - acum_x)
    cdec = jnp.exp(acum_t[:, ln - 1:ln])
    causal = (lax.broadcasted_iota(jnp.int32, (ln, ln), 1)
              <= lax.broadcasted_iota(jnp.int32, (ln, ln), 0))
    lane = lax.broadcasted_iota(jnp.int32, (1, LANES), 1)

    y_parts = []
    for g in range(n_groups):
        bg = xc_ref[:, dssd + g * n_state:dssd + (g + 1) * n_state]
        cg = xc_ref[:, dssd + (n_groups + g) * n_state:dssd + (n_groups + g + 1) * n_state]
        gw = hpg * p_dim
        rows = slice(g * gw, (g + 1) * gw)
        st_g = st_ref[rows, :]
        cb = _dot_nt(cg, bg)
        y_off = _dot_nt(cg, st_g) * jnp.exp(acum_x[:, rows])
        for q in range(hpg // per):
            lsl = slice(g * gw + q * LANES, g * gw + (q + 1) * LANES)
            xdt_q = xdt[:, lsl].astype(BF16)
            y_q = jnp.zeros((ln, LANES), F32)
            for hh in range(per):
                h = g * hpg + q * per + hh
                seg = acum[:, h:h + 1] - acum_t[h:h + 1, :]
                m = cb * jnp.exp(jnp.where(causal, seg, -jnp.inf))
                yd = jnp.dot(m.astype(BF16), xdt_q, preferred_element_type=F32)
                y_q = jnp.where(lane // p_dim == hh, yd, y_q)
            y_parts.append(y_q + y_off[:, q * LANES:(q + 1) * LANES])
        cs = jnp.dot(xdt_end[:, rows].T.astype(BF16), bg.astype(BF16), preferred_element_type=F32)
        for hh in range(hpg):
            h = g * hpg + hh
            hr = slice(h * p_dim, (h + 1) * p_dim)
            dec = jnp.broadcast_to(cdec[h:h + 1, :], (p_dim, n_state))
            st_ref[hr, :] = st_ref[hr, :] * dec + cs[hh * p_dim:(hh + 1) * p_dim, :]
    y = jnp.concatenate(y_parts, axis=1)
    y_ref[...] = _gated_group_norm(y, xs, z_ref[...], dskip_ref[...], gn_ref[...], n_groups)
    hl_ref[...] = st_ref[...]


def _ssd_prompt_call(xc, dt, z, alog_pad, dskip_x, gnorm, *, nhs, p_dim, n_state, n_groups):
    t, dconv = xc.shape
    dssd = nhs * p_dim
    ln = SSD_CHUNK
    assert t % ln == 0 and LANES % p_dim == 0 and (nhs // n_groups) % (LANES // p_dim) == 0
    const = lambda i: (0, 0)
    kern = functools.partial(_ssd_prompt_kernel, nhs=nhs, p_dim=p_dim, n_state=n_state,
                             n_groups=n_groups)
    return pl.pallas_call(
        kern,
        grid=(t // ln,),
        in_specs=[pl.BlockSpec((ln, dconv), lambda i: (i, 0)),
                  pl.BlockSpec((ln, LANES), lambda i: (i, 0)),
                  pl.BlockSpec((ln, dssd), lambda i: (i, 0)),
                  pl.BlockSpec((1, LANES), const), pl.BlockSpec((1, dssd), const),
                  pl.BlockSpec((1, dssd), const)],
        out_specs=[pl.BlockSpec((ln, dssd), lambda i: (i, 0)),
                   pl.BlockSpec((dssd, n_state), const)],
        out_shape=[jax.ShapeDtypeStruct((t, dssd), BF16), jax.ShapeDtypeStruct((dssd, n_state), F32)],
        scratch_shapes=[pltpu.VMEM((dssd, n_state), F32)],
        compiler_params=_cparams("arbitrary"),
        name="ssd_prompt",
    )(xc, dt, z, alog_pad, dskip_x, gnorm)


def _ssd_sample_kernel(xc_ref, dt_ref, z_ref, alog_ref, dskip_ref, gn_ref, st_ref, y_ref, so_ref,
                       *, nhs, p_dim, n_state, n_groups):
    bb = xc_ref.shape[0]
    dssd = nhs * p_dim
    gw = dssd // n_groups
    xs = xc_ref[:, 0:dssd]
    dt = dt_ref[...]
    a = -jnp.exp(alog_ref[...])
    expand = _expand_matrix(LANES, nhs, p_dim)
    dec_x = _xdot_l(jnp.exp(dt * a), expand)
    xdt = xs * _xdot_l(dt, expand)

    def column(row):
        return jnp.broadcast_to(row, (n_state, dssd)).T

    ys = []
    for r in range(bb):
        bfull = jnp.concatenate(
            [jnp.broadcast_to(xc_ref[r:r + 1, dssd + g * n_state:dssd + (g + 1) * n_state],
                              (gw, n_state)) for g in range(n_groups)], axis=0)
        cfull = jnp.concatenate(
            [jnp.broadcast_to(xc_ref[r:r + 1, dssd + (n_groups + g) * n_state:
                                     dssd + (n_groups + g + 1) * n_state],
                              (gw, n_state)) for g in range(n_groups)], axis=0)
        s_new = st_ref[r] * column(dec_x[r:r + 1, :]) + column(xdt[r:r + 1, :]) * bfull
        so_ref[r] = s_new
        ycol = jnp.sum(s_new * cfull, axis=1, keepdims=True)
        ys.append(jnp.broadcast_to(ycol, (dssd, n_state)).T[0:1, :])
    y = jnp.concatenate(ys, axis=0)
    y_ref[...] = _gated_group_norm(y, xs, z_ref[...], dskip_ref[...], gn_ref[...], n_groups)


def _ssd_sample_call(xc, dt, z, alog_pad, dskip_x, gnorm, state, *, nhs, p_dim, n_state, n_groups):
    db, dconv = xc.shape
    dssd = nhs * p_dim
    bb = SUBLANES
    assert db % bb == 0 and n_state == LANES
    const = lambda i: (0, 0)
    kern = functools.partial(_ssd_sample_kernel, nhs=nhs, p_dim=p_dim, n_state=n_state,
                             n_groups=n_groups)
    return pl.pallas_call(
        kern,
        grid=(db // bb,),
        in_specs=[pl.BlockSpec((bb, dconv), lambda i: (i, 0)),
                  pl.BlockSpec((bb, LANES), lambda i: (i, 0)),
                  pl.BlockSpec((bb, dssd), lambda i: (i, 0)),
                  pl.BlockSpec((1, LANES), const), pl.BlockSpec((1, dssd), const),
                  pl.BlockSpec((1, dssd), const),
                  pl.BlockSpec((bb, dssd, n_state), lambda i: (i, 0, 0))],
        out_specs=[pl.BlockSpec((bb, dssd), lambda i: (i, 0)),
                   pl.BlockSpec((bb, dssd, n_state), lambda i: (i, 0, 0))],
        out_shape=[jax.ShapeDtypeStruct((db, dssd), BF16),
                   jax.ShapeDtypeStruct((db, dssd, n_state), F32)],
        compiler_params=_cparams("parallel"),
        name="ssd_sample",
    )(xc, dt, z, alog_pad, dskip_x, gnorm, state)


def _pad_lanes(v):
    return jnp.zeros((1, LANES), F32).at[0, :v.shape[0]].set(v)


def kernel(x_prompt, x_sample, cache_k, cache_v, cache_logf, state_ssm, state_conv, page_table, c_prompt, c_sample, w_ada, b_ada, norm_ffn1, w_ffn1_in, w_ffn1_out, norm_mix, w_in, b_forget, conv_w, conv_b, dt_bias, a_log, d_skip, ssd_norm, w_out, norm_ffn2, w_ffn2_in, w_ffn2_out, norm_final):
    bsz, t, d = x_prompt.shape
    db, s_new, _ = x_sample.shape
    depth, n_phys, page, nh, hd = cache_k.shape
    _, _, nhs, p_dim, n_state = state_ssm.shape
    conv_w1, dconv = state_conv.shape[2:]
    datt, dssd = nh * hd, nhs * p_dim
    n_groups = (dconv - dssd) // (2 * n_state)
    assert bsz == 1 and s_new == 1 and db % SUBLANES == 0 and c_prompt.shape[0] == 1
    scale = hd ** -0.5
    prompt_row_block = db // SUBLANES
    tm = min(512, t)
    n_pages = page_table.shape[1]
    pg = min(16, n_pages)

    xp = x_prompt.reshape(t, d)
    xs_ = x_sample.reshape(db, d)
    ckt = jnp.transpose(cache_k, (0, 1, 3, 4, 2)).reshape(depth * n_phys, datt, page)
    cvt = jnp.transpose(cache_v, (0, 1, 3, 4, 2)).reshape(depth * n_phys, datt, page)
    clft = jnp.transpose(cache_logf, (0, 1, 3, 2)).reshape(depth * n_phys, nh, page)
    heads_last = lambda at, n: jnp.transpose(at.reshape(nh, hd, n), (2, 0, 1))
    c_all = jnp.concatenate([c_sample, c_prompt, jnp.zeros((SUBLANES - 1, d), F32)], axis=0)

    outs_p, outs_s = [], []
    for l in range(depth):
        sp = (datt, 2 * datt, 3 * datt, 3 * datt + nh, 3 * datt + nh + dssd,
              3 * datt + nh + dssd + dconv)
        wi = w_in[l]
        wqkv = wi[:, :sp[2]].astype(BF16)
        wzx = wi[:, sp[3]:sp[5]].astype(BF16)
        wfd = (jnp.zeros((d, 2 * LANES), F32).at[:, :nh].set(wi[:, sp[2]:sp[3]])
               .at[:, LANES:LANES + nhs].set(wi[:, sp[5]:])).astype(BF16)
        bfd = jnp.concatenate([_pad_lanes(b_forget[l]), _pad_lanes(dt_bias[l])], axis=1)
        alog_pad = _pad_lanes(a_log[l])
        dskip_x = jnp.repeat(d_skip[l], p_dim).reshape(1, dssd)
        gnorm = ssd_norm[l].reshape(1, dssd)
        w1i, w1o = w_ffn1_in[l].astype(BF16), w_ffn1_out[l].astype(BF16)
        w2i, w2o = w_ffn2_in[l].astype(BF16), w_ffn2_out[l].astype(BF16)
        wo = w_out[l].astype(BF16)
        last = l == depth - 1

        m_all = _ada_call(c_all, w_ada[l], b_ada[l])
        proj = dict(datt=datt, dssd=dssd, nh=nh, nhs=nhs, scale=scale)
        ssd = dict(nhs=nhs, p_dim=p_dim, n_state=n_state, n_groups=n_groups)

        xp = _ffn_call(xp, m_all, 0, norm_ffn1[l], w1i, w1o, per_row=False, tm=tm,
                       prompt_row_block=prompt_row_block, name="ffn1_prompt")
        (qa, ka, vbt, kt, vt, lft, z, xc, dt, tail) = _inproj_prompt_call(
            xp, m_all, prompt_row_block, norm_mix[l], wqkv, wzx, wfd, bfd, conv_w[l], conv_b[l],
            tm=tm, hd=hd, **proj)
        att = _attn_prompt_call(qa, ka, vbt, nh=nh, hd=hd, tq=tm)
        yssd, h_last = _ssd_prompt_call(xc, dt, z, alog_pad, dskip_x, gnorm, **ssd)
        xp = _ffn_call(xp, m_all, 6, norm_ffn2[l], w2i, w2o, per_row=False, tm=tm,
                       prompt_row_block=prompt_row_block, mix=(att, yssd, wo, 5),
                       final_norm_w=norm_final if last else None, name="ffn2_prompt")
        outs_p.append((heads_last(kt, t)[None], heads_last(vt, t)[None], lft.T[None],
                       h_last.reshape(1, nhs, p_dim, n_state),
                       tail[SUBLANES - conv_w1:].reshape(1, conv_w1, dconv)))

        xs_ = _ffn_call(xs_, m_all, 0, norm_ffn1[l], w1i, w1o, per_row=True, tm=db,
                        prompt_row_block=prompt_row_block, name="ffn1_sample")
        (q, k, v, kt, vt, logf, lft, z, xc, dt, conv_new) = _inproj_sample_call(
            xs_, m_all, norm_mix[l], wqkv, wzx, wfd, bfd, conv_w[l], conv_b[l],
            jnp.transpose(state_conv[l], (1, 0, 2)), **proj)
        att = _attn_sample_call(page_table, q, k, v, logf, ckt, cvt, clft, l * n_phys,
                                nh=nh, hd=hd, pg=pg)
        yssd, st_new = _ssd_sample_call(xc, dt, z, alog_pad, dskip_x, gnorm,
                                        state_ssm[l].reshape(db, dssd, n_state), **ssd)
        xs_ = _ffn_call(xs_, m_all, 6, norm_ffn2[l], w2i, w2o, per_row=True, tm=db,
                        prompt_row_block=prompt_row_block, mix=(att, yssd, wo, 5),
                        final_norm_w=norm_final if last else None, name="ffn2_sample")
        outs_s.append((heads_last(kt, db)[:, None], heads_last(vt, db)[:, None], lft.T[:, None],
                       st_new.reshape(db, nhs, p_dim, n_state), jnp.transpose(conv_new, (1, 0, 2))))

    stack = lambda outs, i: jnp.stack([o[i] for o in outs])
    return (xp.reshape(bsz, t, d), xs_.reshape(db, 1, d),
            stack(outs_p, 0), stack(outs_p, 1), stack(outs_p, 2), stack(outs_p, 3), stack(outs_p, 4),
            stack(outs_s, 0), stack(outs_s, 1), stack(outs_s, 2), stack(outs_s, 3), stack(outs_s, 4))
```

```python
import functools

import math

import jax
import jax.numpy as jnp
import numpy as np
from jax import lax
from jax.experimental import pallas as pl
from jax.experimental.pallas import tpu as pltpu

F32 = jnp.float32
BF16 = jnp.bfloat16
EPS = 1e-6
LOG2E = math.log2(math.e)
LANES = 128
SUBLANES = 8
SSD_CHUNK = 128
VMEM_LIMIT = 56 * 1024 * 1024


def _cparams(*semantics):
    return pltpu.CompilerParams(dimension_semantics=semantics, vmem_limit_bytes=VMEM_LIMIT)


def _dot(a, b):
    return jnp.dot(a.astype(BF16), b.astype(BF16), preferred_element_type=F32)


def _dot_nt(a, b):
    return lax.dot_general(a.astype(BF16), b.astype(BF16), (((1,), (1,)), ((), ())),
                           preferred_element_type=F32)


def _split3(a):
    hi = a.astype(BF16)
    r = a - hi.astype(F32)
    mid = r.astype(BF16)
    lo = (r - mid.astype(F32)).astype(BF16)
    return hi, mid, lo


def _xdot_l(a, e):
    hi, mid, lo = _split3(a)
    f = functools.partial(jnp.dot, preferred_element_type=F32)
    return f(hi, e) + f(mid, e) + f(lo, e)


def _xdot_r(e, b):
    hi, mid, lo = _split3(b)
    f = functools.partial(jnp.dot, preferred_element_type=F32)
    return f(e, hi) + f(e, mid) + f(e, lo)


def _silu(x):
    return x * jax.nn.sigmoid(x)


def _softplus(x):
    return jnp.maximum(x, 0.0) + jnp.log1p(jnp.exp(-jnp.abs(x)))


def _rms(x):
    return x * lax.rsqrt(jnp.mean(x * x, axis=-1, keepdims=True) + EPS)


def _norm_mod(x, nw, shift, scale):
    return (_rms(x) * nw) * (1.0 + scale) + shift


def _expand_matrix(n_rows, n_heads, width):
    row = lax.broadcasted_iota(jnp.int32, (n_rows, n_heads * width), 0)
    col = lax.broadcasted_iota(jnp.int32, (n_rows, n_heads * width), 1)
    return (col // width == row).astype(BF16)


def _tri_matrix(n):
    row = lax.broadcasted_iota(jnp.int32, (n, n), 0)
    col = lax.broadcasted_iota(jnp.int32, (n, n), 1)
    return (col <= row).astype(BF16)


def _gated_group_norm(y, xs, z, dskip, gnorm, n_groups):
    y = (y + dskip * xs) * _silu(z)
    gw = y.shape[1] // n_groups
    parts = [_rms(y[:, g * gw:(g + 1) * gw]) for g in range(n_groups)]
    return (jnp.concatenate(parts, axis=1) * gnorm).astype(BF16)


def _ada_kernel(c_ref, w_ref, b_ref, o_ref):
    o_ref[...] = _dot(_silu(c_ref[...]), w_ref[...]) + b_ref[...]


def _ada_call(c_all, w, b):
    r, d = c_all.shape
    n = w.shape[1]
    tn = d
    return pl.pallas_call(
        _ada_kernel,
        grid=(n // tn,),
        in_specs=[pl.BlockSpec((r, d), lambda j: (0, 0)),
                  pl.BlockSpec((d, tn), lambda j: (0, j)),
                  pl.BlockSpec((1, tn), lambda j: (0, j))],
        out_specs=pl.BlockSpec((r, tn), lambda j: (0, j)),
        out_shape=jax.ShapeDtypeStruct((r, n), F32),
        compiler_params=_cparams("parallel"),
        name="ada",
    )(c_all, w, b.reshape(1, n))


def _mod_spec(per_row, tm, d, idx, prompt_row_block):
    if per_row:
        return pl.BlockSpec((tm, d), lambda i: (i, idx))
    return pl.BlockSpec((SUBLANES, d), lambda i: (prompt_row_block, idx))


def _ffn_kernel(*refs, per_row, with_mix, final_norm, ff, tf, datt):
    it = iter(refs)
    x_ref = next(it)
    if with_mix:
        att_ref, ys_ref, wo_ref, g2_ref = next(it), next(it), next(it), next(it)
    sh_ref, sc_ref, g_ref, nw_ref, win_ref, wout_ref = (next(it) for _ in range(6))
    nf_ref = next(it) if final_norm else None
    o_ref = next(it)
    a_ref = next(it)
    rows = slice(None) if per_row else slice(0, 1)

    x = x_ref[...]
    if with_mix:
        mix = (jnp.dot(att_ref[...].astype(BF16), wo_ref[0:datt, :], preferred_element_type=F32)
               + jnp.dot(ys_ref[...], wo_ref[datt:, :], preferred_element_type=F32))
        x = x + g2_ref[rows, :] * mix
    h = _norm_mod(x, nw_ref[...], sh_ref[rows, :], sc_ref[rows, :]).astype(BF16)
    for c in range(ff // tf):
        g = jnp.dot(h, win_ref[:, c * tf:(c + 1) * tf], preferred_element_type=F32)
        u = jnp.dot(h, win_ref[:, ff + c * tf:ff + (c + 1) * tf], preferred_element_type=F32)
        a_ref[:, c * tf:(c + 1) * tf] = (_silu(g) * u).astype(BF16)
    y = x + (0.5 * g_ref[rows, :]) * jnp.dot(a_ref[...], wout_ref[...], preferred_element_type=F32)
    if final_norm:
        y = _rms(y) * nf_ref[...]
    o_ref[...] = y


def _ffn_call(x, m_all, ada_base, norm_w, w_in_b, w_out_b, *, per_row, tm, prompt_row_block,
              mix=None, final_norm_w=None, name):
    r, d = x.shape
    ff = w_out_b.shape[0]
    tf = 256
    assert r % tm == 0 and ff % tf == 0
    const = lambda i: (0, 0)
    resident = functools.partial(pl.BlockSpec, index_map=const, pipeline_mode=pl.Buffered(1))
    mspec = functools.partial(_mod_spec, per_row, tm, d, prompt_row_block=prompt_row_block)
    args, specs = [x], [pl.BlockSpec((tm, d), lambda i: (i, 0))]
    datt = 0
    if mix is not None:
        att, ys, wo_b, g2_idx = mix
        datt = att.shape[1]
        args += [att, ys, wo_b, m_all]
        specs += [pl.BlockSpec((tm, datt), lambda i: (i, 0)),
                  pl.BlockSpec((tm, ys.shape[1]), lambda i: (i, 0)),
                  resident(wo_b.shape), mspec(g2_idx)]
    args += [m_all, m_all, m_all, norm_w.reshape(1, d), w_in_b, w_out_b]
    specs += [mspec(ada_base), mspec(ada_base + 1), mspec(ada_base + 2),
              pl.BlockSpec((1, d), const), resident(w_in_b.shape), resident(w_out_b.shape)]
    if final_norm_w is not None:
        args.append(final_norm_w.reshape(1, d))
        specs.append(pl.BlockSpec((1, d), const))
    kern = functools.partial(_ffn_kernel, per_row=per_row, with_mix=mix is not None,
                             final_norm=final_norm_w is not None, ff=ff, tf=tf, datt=datt)
    return pl.pallas_call(
        kern,
        grid=(r // tm,),
        in_specs=specs,
        out_specs=pl.BlockSpec((tm, d), lambda i: (i, 0)),
        out_shape=jax.ShapeDtypeStruct((r, d), F32),
        scratch_shapes=[pltpu.VMEM((tm, ff), BF16)],
        compiler_params=_cparams("parallel"),
        name=name,
    )(*args)


def _project(x, sh, sc, nw_ref, wqkv_ref, wzx_ref, wfd_ref, bfd_ref, *, datt, dssd, nh, nhs):
    h = _norm_mod(x, nw_ref[...], sh, sc).astype(BF16)
    qkv = jnp.dot(h, wqkv_ref[...], preferred_element_type=F32)
    zx = jnp.dot(h, wzx_ref[...], preferred_element_type=F32)
    fd = jnp.dot(h, wfd_ref[...], preferred_element_type=F32) + bfd_ref[...]
    lane = lax.broadcasted_iota(jnp.int32, (1, LANES), 1)
    logf = jnp.where(lane < nh, -_softplus(-fd[:, :LANES]), 0.0)
    dt = jnp.where(lane < nhs, _softplus(fd[:, LANES:]), 0.0)
    return (qkv[:, :datt], qkv[:, datt:2 * datt], qkv[:, 2 * datt:], zx[:, :dssd], zx[:, dssd:],
            logf, dt)


def _inproj_prompt_kernel(x_ref, sh_ref, sc_ref, nw_ref, wqkv_ref, wzx_ref, wfd_ref, bfd_ref,
                          cw_ref, cb_ref, selq_ref, qone_ref, selk_ref, kone_ref,
                          qa_ref, ka_ref, vbt_ref, kt_ref, vt_ref, lft_ref, z_ref,
                          xc_ref, dt_ref, tail_ref,
                          ptail_ref, cf_ref, *, datt, dssd, nh, nhs, hd, scale, conv_w):
    @pl.when(pl.program_id(0) == 0)
    def _():
        ptail_ref[...] = jnp.zeros_like(ptail_ref)
        cf_ref[...] = jnp.zeros_like(cf_ref)

    tm = x_ref.shape[0]
    q, k, v, z, xbc, logf, dt = _project(
        x_ref[...], sh_ref[0:1, :], sc_ref[0:1, :], nw_ref, wqkv_ref, wzx_ref, wfd_ref, bfd_ref,
        datt=datt, dssd=dssd, nh=nh, nhs=nhs)
    kt_ref[...] = k.T
    vt = v.T
    vt_ref[...] = vt
    vbt_ref[...] = vt.astype(BF16)
    z_ref[...] = z
    lft_ref[...] = logf.T[0:SUBLANES, :]
    dt_ref[...] = dt

    fc = _xdot_r(_tri_matrix(tm), logf) + cf_ref[...]
    cf_ref[...] = fc[tm - 1:tm, :]

    per = LANES // hd
    lane = lax.broadcasted_iota(jnp.int32, (1, LANES), 1)
    f3 = jnp.concatenate(_split3(fc * LOG2E), axis=1)
    qs = q * (scale * LOG2E)
    qbias = (jnp.dot(f3, selq_ref[...], preferred_element_type=F32) + qone_ref[...]).astype(BF16)
    kbias = (jnp.dot(f3, selk_ref[...], preferred_element_type=F32) + kone_ref[...]).astype(BF16)
    for h in range(nh):
        g, hh = divmod(h, per)
        qpair = qs[:, g * LANES:(g + 1) * LANES]
        qa_ref[:, 2 * h * LANES:(2 * h + 1) * LANES] = jnp.where(lane // hd == hh, qpair, 0.0).astype(BF16)
        qa_ref[:, (2 * h + 1) * LANES:(2 * h + 2) * LANES] = qbias[:, h * LANES:(h + 1) * LANES]
    for g in range(nh // per):
        ka_ref[:, 2 * g * LANES:(2 * g + 1) * LANES] = k[:, g * LANES:(g + 1) * LANES].astype(BF16)
        ka_ref[:, (2 * g + 1) * LANES:(2 * g + 2) * LANES] = kbias[:, g * LANES:(g + 1) * LANES]

    ptail = ptail_ref[...]
    row8 = lax.broadcasted_iota(jnp.int32, ptail.shape, 0)
    acc = cb_ref[...]
    for i in range(conv_w):
        s = conv_w - 1 - i
        if s == 0:
            xs = xbc
        else:
            r = pltpu.roll(xbc, s, 0)
            top = jnp.where(row8 < s, pltpu.roll(ptail, s, 0), r[0:SUBLANES])
            xs = jnp.concatenate([top, r[SUBLANES:]], axis=0)
        acc = acc + xs * cw_ref[i:i + 1, :]
    xc_ref[...] = _silu(acc)
    tail = xbc[tm - SUBLANES:tm]
    ptail_ref[...] = tail
    tail_ref[...] = tail


def _bias_selectors(nh, hd):
    per = LANES // hd
    selq = np.zeros((3 * LANES, nh, LANES), np.float32)
    qone = np.zeros((nh, LANES), np.float32)
    selk = np.zeros((3 * LANES, nh // per, LANES), np.float32)
    kone = np.zeros((nh // per, LANES), np.float32)
    kone[:, 0:3] = 1.0
    for h in range(nh):
        g, hh = divmod(h, per)
        for c in range(3):
            selq[c * LANES + h, h, c] = 1.0
            selk[c * LANES + h, g, 3 + 3 * hh + c] = -1.0
        qone[h, 3 + 3 * hh:6 + 3 * hh] = 1.0
    return (jnp.asarray(selq.reshape(3 * LANES, -1), BF16), jnp.asarray(qone.reshape(1, -1), F32),
            jnp.asarray(selk.reshape(3 * LANES, -1), BF16), jnp.asarray(kone.reshape(1, -1), F32))


def _inproj_prompt_call(x, m_all, prompt_row_block, norm_w, wqkv, wzx, wfd, bfd, cw, cb, *,
                        datt, dssd, nh, nhs, hd, scale, tm):
    t, d = x.shape
    dconv = cw.shape[1]
    per = LANES // hd
    assert t % tm == 0 and nh == SUBLANES and conv_w_ok(cw.shape[0]) and 3 + 3 * per <= LANES
    const = lambda i: (0, 0)
    full = lambda a: pl.BlockSpec(a.shape, lambda i: (0,) * a.ndim)
    rowblk = lambda w: pl.BlockSpec((tm, w), lambda i: (i, 0))
    colblk = lambda h: pl.BlockSpec((h, tm), lambda i: (0, i))
    mspec = functools.partial(_mod_spec, False, tm, d, prompt_row_block=prompt_row_block)
    kern = functools.partial(_inproj_prompt_kernel, datt=datt, dssd=dssd, nh=nh, nhs=nhs, hd=hd,
                             scale=scale, conv_w=cw.shape[0])
    sel = _bias_selectors(nh, hd)
    sds = jax.ShapeDtypeStruct
    return pl.pallas_call(
        kern,
        grid=(t // tm,),
        in_specs=[rowblk(d), mspec(3), mspec(4), pl.BlockSpec((1, d), const),
                  full(wqkv), full(wzx), full(wfd), full(bfd), full(cw),
                  pl.BlockSpec((1, dconv), const)] + [full(a) for a in sel],
        out_specs=[rowblk(2 * nh * LANES), rowblk(2 * (nh // per) * LANES), colblk(datt),
                   colblk(datt), colblk(datt), colblk(SUBLANES),
                   rowblk(dssd), rowblk(dconv), rowblk(LANES),
                   pl.BlockSpec((SUBLANES, dconv), const)],
        out_shape=[sds((t, 2 * nh * LANES), BF16), sds((t, 2 * (nh // per) * LANES), BF16),
                   sds((datt, t), BF16), sds((datt, t), F32), sds((datt, t), F32),
                   sds((SUBLANES, t), F32), sds((t, dssd), F32),
                   sds((t, dconv), F32), sds((t, LANES), F32), sds((SUBLANES, dconv), F32)],
        scratch_shapes=[pltpu.VMEM((SUBLANES, dconv), F32), pltpu.VMEM((1, LANES), F32)],
        compiler_params=_cparams("arbitrary"),
        name="inproj_prompt",
    )(x, m_all, m_all, norm_w.reshape(1, d), wqkv, wzx, wfd, bfd, cw, cb.reshape(1, dconv), *sel)


def conv_w_ok(w):
    return 1 <= w - 1 < SUBLANES


def _inproj_sample_kernel(x_ref, sh_ref, sc_ref, nw_ref, wqkv_ref, wzx_ref, wfd_ref, bfd_ref,
                          cw_ref, cb_ref, cs_ref,
                          q_ref, k_ref, v_ref, kt_ref, vt_ref, lf_ref, lft_ref, z_ref, xc_ref, dt_ref,
                          cn_ref, *, datt, dssd, nh, nhs, scale, conv_w):
    q, k, v, z, xbc, logf, dt = _project(
        x_ref[...], sh_ref[...], sc_ref[...], nw_ref, wqkv_ref, wzx_ref, wfd_ref, bfd_ref,
        datt=datt, dssd=dssd, nh=nh, nhs=nhs)
    q_ref[...] = q * scale
    k_ref[...] = k
    v_ref[...] = v
    kt_ref[...] = k.T
    vt_ref[...] = v.T
    z_ref[...] = z
    lf_ref[...] = logf[:, 0:nh]
    lft_ref[...] = logf.T[0:SUBLANES, :]
    dt_ref[...] = dt
    acc = cb_ref[...]
    for i in range(conv_w - 1):
        acc = acc + cs_ref[i] * cw_ref[i:i + 1, :]
    acc = acc + xbc * cw_ref[conv_w - 1:conv_w, :]
    xc_ref[...] = _silu(acc)
    for i in range(conv_w - 2):
        cn_ref[i] = cs_ref[i + 1]
    cn_ref[conv_w - 2] = xbc


def _inproj_sample_call(x, m_all, norm_w, wqkv, wzx, wfd, bfd, cw, cb, conv_state, *,
                        datt, dssd, nh, nhs, scale):
    r, d = x.shape
    conv_w, dconv = cw.shape
    const = lambda i: (0, 0)
    full = lambda a: pl.BlockSpec(a.shape, lambda i: (0,) * a.ndim)
    mspec = functools.partial(_mod_spec, True, r, d, prompt_row_block=0)
    kern = functools.partial(_inproj_sample_kernel, datt=datt, dssd=dssd, nh=nh, nhs=nhs,
                             scale=scale, conv_w=conv_w)
    sds = jax.ShapeDtypeStruct
    shapes = [sds((r, datt), F32), sds((r, datt), F32), sds((r, datt), F32), sds((datt, r), F32),
              sds((datt, r), F32), sds((r, nh), F32), sds((SUBLANES, r), F32), sds((r, dssd), F32),
              sds((r, dconv), F32), sds((r, LANES), F32), sds(conv_state.shape, F32)]
    return pl.pallas_call(
        kern,
        grid=(1,),
        in_specs=[full(x), mspec(3), mspec(4), pl.BlockSpec((1, d), const), full(wqkv), full(wzx),
                  full(wfd), full(bfd), full(cw), pl.BlockSpec((1, dconv), const), full(conv_state)],
        out_specs=[full(s) for s in shapes],
        out_shape=shapes,
        compiler_params=_cparams("arbitrary"),
        name="inproj_sample",
    )(x, m_all, m_all, norm_w.reshape(1, d), wqkv, wzx, wfd, bfd, cw, cb.reshape(1, dconv), conv_state)


def _attn_prompt_kernel(qi_ref, kj_ref, q_ref, k_ref, vt_ref, o_ref, m_ref, l_ref, acc_ref, *, nh, hd):
    step_id = pl.program_id(0)
    i = qi_ref[step_id]
    j = kj_ref[step_id]
    tq, tk = q_ref.shape[0], k_ref.shape[0]
    per = LANES // hd
    gw = 2 * LANES

    @pl.when(j == 0)
    def _():
        m_ref[...] = jnp.full_like(m_ref, -jnp.inf)
        l_ref[...] = jnp.zeros_like(l_ref)
        acc_ref[...] = jnp.zeros_like(acc_ref)

    ones = jnp.ones((2 * SUBLANES, tk), BF16)

    def step(masked):
        if masked:
            keep = (lax.broadcasted_iota(jnp.int32, (tk, tq), 0)
                    <= lax.broadcasted_iota(jnp.int32, (tk, tq), 1))
        def scores(h):
            g = h // per
            return lax.dot_general(k_ref[:, g * gw:(g + 1) * gw], q_ref[:, h * gw:(h + 1) * gw],
                                   (((1,), (1,)), ((), ())), preferred_element_type=F32)

        def softmax(h, st):
            if masked:
                st = jnp.where(keep, st, -jnp.inf)
            m_old = m_ref[h:h + 1, :]
            m_new = jnp.maximum(m_old, jnp.max(st, axis=0, keepdims=True))
            m_ref[h:h + 1, :] = m_new
            return jnp.exp2(st - m_new).astype(BF16), jnp.exp2(m_old - m_new)

        def values(h, p, alpha):
            hr = slice(h * hd, (h + 1) * hd)
            pv = jnp.dot(jnp.concatenate([vt_ref[hr, :], ones], axis=0), p,
                         preferred_element_type=F32)
            acc_ref[hr, :] = alpha * acc_ref[hr, :] + pv[0:hd]
            l_ref[h:h + 1, :] = alpha * l_ref[h:h + 1, :] + pv[hd:hd + 1]

        st = {0: scores(0)}
        pa = {}
        for h in range(nh + 1):
            if h + 1 < nh:
                st[h + 1] = scores(h + 1)
            if h < nh:
                pa[h] = softmax(h, st.pop(h))
            if h >= 1:
                values(h - 1, *pa.pop(h - 1))

    @pl.when(j < i)
    def _():
        step(False)

    @pl.when(j == i)
    def _():
        step(True)
        out_t = jnp.concatenate(
            [acc_ref[h * hd:(h + 1) * hd, :] * (1.0 / l_ref[h:h + 1, :]) for h in range(nh)], axis=0)
        o_ref[...] = out_t.T.astype(BF16)


def _attn_prompt_call(qa, ka, vbt, *, nh, hd, tq):
    t = qa.shape[0]
    datt = nh * hd
    per = LANES // hd
    assert t % tq == 0 and LANES % hd == 0 and nh % per == 0
    nq = t // tq
    qi = np.concatenate([np.full(i + 1, i, np.int32) for i in range(nq)])
    kj = np.concatenate([np.arange(i + 1, dtype=np.int32) for i in range(nq)])
    kern = functools.partial(_attn_prompt_kernel, nh=nh, hd=hd)
    return pl.pallas_call(
        kern,
        grid_spec=pltpu.PrefetchScalarGridSpec(
            num_scalar_prefetch=2,
            grid=(len(qi),),
            in_specs=[pl.BlockSpec((tq, qa.shape[1]), lambda s, qi, kj: (qi[s], 0)),
                      pl.BlockSpec((tq, ka.shape[1]), lambda s, qi, kj: (kj[s], 0)),
                      pl.BlockSpec((datt, tq), lambda s, qi, kj: (0, kj[s]))],
            out_specs=pl.BlockSpec((tq, datt), lambda s, qi, kj: (qi[s], 0)),
            scratch_shapes=[pltpu.VMEM((nh, tq), F32), pltpu.VMEM((nh, tq), F32),
                            pltpu.VMEM((datt, tq), F32)]),
        out_shape=jax.ShapeDtypeStruct((t, datt), BF16),
        compiler_params=_cparams("arbitrary"),
        name="attn_prompt",
    )(jnp.asarray(qi), jnp.asarray(kj), qa, ka, vbt)


def _attn_sample_kernel(pt_ref, q_ref, kn_ref, vn_ref, lfn_ref, *rest, nh, hd, pg, sub):
    k_refs, v_refs, lf_refs = rest[0:pg], rest[pg:2 * pg], rest[2 * pg:3 * pg]
    o_ref, qcol_ref, m_ref, l_ref, acc_ref, cf_ref = rest[3 * pg:]
    j = pl.program_id(1)
    datt, page = k_refs[0].shape

    def head_rows(a):
        return jnp.concatenate([jnp.broadcast_to(a[h:h + 1, :], (hd, a.shape[1])) for h in range(nh)],
                               axis=0)

    @pl.when(j == 0)
    def _():
        qcol_ref[...] = jnp.broadcast_to(q_ref[...], (page, datt)).T
        m_ref[...] = jnp.full_like(m_ref, -jnp.inf)
        l_ref[...] = jnp.zeros_like(l_ref)
        acc_ref[...] = jnp.zeros_like(acc_ref)
        cf_ref[...] = jnp.zeros_like(cf_ref)

    triu = (lax.broadcasted_iota(jnp.int32, (page, page), 0)
            <= lax.broadcasted_iota(jnp.int32, (page, page), 1)).astype(BF16)
    f_loc = _xdot_l(jnp.concatenate([r[...] for r in lf_refs], axis=0), triu)
    f_tot = jnp.broadcast_to(f_loc[:, page - 1:page], f_loc.shape)
    carry = [cf_ref[...]]

    def scores(pages):
        out = []
        for p in pages:
            s = jnp.concatenate(
                [jnp.sum(k_refs[p][h * hd:(h + 1) * hd, :] * qcol_ref[h * hd:(h + 1) * hd, :],
                         axis=0, keepdims=True) for h in range(nh)], axis=0)
            out.append(s - (f_loc[p * nh:(p + 1) * nh, :] + carry[0]))
            carry[0] = carry[0] + f_tot[p * nh:(p + 1) * nh, :]
        return out

    groups = [range(g, min(g + sub, pg)) for g in range(0, pg, sub)]
    m_new, l_new = m_ref[...], l_ref[...]
    sc_next = scores(groups[0])
    for gi, pages in enumerate(groups):
        sc = sc_next
        m_old = m_new
        for s in sc:
            m_new = jnp.maximum(m_new, jnp.max(s, axis=1, keepdims=True))
        alpha = jnp.exp(m_old - m_new)
        l_new = alpha * l_new
        probs = []
        for s in sc:
            pr = jnp.exp(s - m_new)
            l_new = l_new + jnp.sum(pr, axis=1, keepdims=True)
            probs.append(pr)
        if gi + 1 < len(groups):
            sc_next = scores(groups[gi + 1])
        for h in range(nh):
            hr = slice(h * hd, (h + 1) * hd)
            a = acc_ref[hr, :] * alpha[h:h + 1, :]
            for p, pr in zip(pages, probs):
                a = a + v_refs[p][hr, :] * pr[h:h + 1, :]
            acc_ref[hr, :] = a
    cf = carry[0]
    cf_ref[...] = cf
    m_ref[...] = m_new
    l_ref[...] = l_new

    @pl.when(j == pl.num_programs(1) - 1)
    def _():
        row = lax.broadcasted_iota(jnp.int32, (nh, datt), 0)
        col = lax.broadcasted_iota(jnp.int32, (nh, datt), 1)
        qk = jnp.where(col // hd == row, jnp.broadcast_to(q_ref[...] * kn_ref[...], (nh, datt)), 0.0)
        s_new = jnp.sum(qk, axis=1, keepdims=True) - (cf[:, 0:1] + lfn_ref[...])
        m_fin = jnp.maximum(m_new, s_new)
        a_fin = jnp.exp(m_new - m_fin)
        p_new = jnp.exp(s_new - m_fin)
        l_fin = a_fin * l_new + p_new
        tot = jnp.sum(acc_ref[...], axis=1, keepdims=True)
        vcol = jnp.broadcast_to(vn_ref[...], (LANES, datt)).T[:, 0:1]
        out = (tot * head_rows(a_fin) + head_rows(p_new.astype(BF16).astype(F32)) * vcol) \
            / head_rows(l_fin)
        o_ref[...] = jnp.broadcast_to(out, (datt, LANES)).T[0:1, :]


def _attn_sample_call(page_table, q, kn, vn, lfn, ckt, cvt, clft, layer_base, *, nh, hd, pg):
    db, datt = q.shape
    n_pages = page_table.shape[1]
    page = ckt.shape[2]
    assert n_pages % pg == 0
    row3 = lambda a: a.reshape(db, 1, a.shape[1])
    rowspec = lambda w: pl.BlockSpec((None, 1, w), lambda b, j, pt: (b, 0, 0))

    def page_spec(p, rows):
        return pl.BlockSpec((None, rows, page), lambda b, j, pt: (layer_base + pt[b, j * pg + p], 0, 0))

    kern = functools.partial(_attn_sample_kernel, nh=nh, hd=hd, pg=pg, sub=pg)
    out = pl.pallas_call(
        kern,
        grid_spec=pltpu.PrefetchScalarGridSpec(
            num_scalar_prefetch=1,
            grid=(db, n_pages // pg),
            in_specs=([rowspec(datt), rowspec(datt), rowspec(datt),
                       pl.BlockSpec((None, nh, 1), lambda b, j, pt: (b, 0, 0))]
                      + [page_spec(p, datt) for p in range(pg)]
                      + [page_spec(p, datt) for p in range(pg)]
                      + [page_spec(p, nh) for p in range(pg)]),
            out_specs=rowspec(datt),
            scratch_shapes=[pltpu.VMEM((datt, page), F32), pltpu.VMEM((nh, 1), F32),
                            pltpu.VMEM((nh, 1), F32), pltpu.VMEM((datt, page), F32),
                            pltpu.VMEM((nh, page), F32)]),
        out_shape=jax.ShapeDtypeStruct((db, 1, datt), F32),
        compiler_params=_cparams("parallel", "arbitrary"),
        name="attn_sample",
    )(page_table, row3(q), row3(kn), row3(vn), lfn.reshape(db, nh, 1),
      *([ckt] * pg), *([cvt] * pg), *([clft] * pg))
    return out.reshape(db, datt)


def _ssd_prompt_kernel(xc_ref, dt_ref, z_ref, alog_ref, dskip_ref, gn_ref, y_ref, hl_ref, st_ref,
                       *, nhs, p_dim, n_state, n_groups):
    @pl.when(pl.program_id(0) == 0)
    def _():
        st_ref[...] = jnp.zeros_like(st_ref)

    ln = xc_ref.shape[0]
    dssd = nhs * p_dim
    hpg = nhs // n_groups
    per = LANES // p_dim
    xs = xc_ref[:, 0:dssd]
    dt = dt_ref[...]
    a = -jnp.exp(alog_ref[...])
    expand = _expand_matrix(LANES, nhs, p_dim)
    acum = _xdot_r(_tri_matrix(ln), dt * a)
    acum_t = acum.T
    acum_x = _xdot_l(acum, expand)
    xdt = xs * _xdot_l(dt, expand)
    a_last_x = acum_x[ln - 1:ln, :]
    xdt_end = xdt * jnp.exp(a_last_x - acum_x)
    cdec = jnp.exp(acum_t[:, ln - 1:ln])
    causal = (lax.broadcasted_iota(jnp.int32, (ln, ln), 1)
              <= lax.broadcasted_iota(jnp.int32, (ln, ln), 0))
    lane = lax.broadcasted_iota(jnp.int32, (1, LANES), 1)

    y_parts = []
    for g in range(n_groups):
        bg = xc_ref[:, dssd + g * n_state:dssd + (g + 1) * n_state]
        cg = xc_ref[:, dssd + (n_groups + g) * n_state:dssd + (n_groups + g + 1) * n_state]
        gw = hpg * p_dim
        rows = slice(g * gw, (g + 1) * gw)
        st_g = st_ref[rows, :]
        cb = _dot_nt(cg, bg)
        y_off = _dot_nt(cg, st_g) * jnp.exp(acum_x[:, rows])
        for q in range(hpg // per):
            lsl = slice(g * gw + q * LANES, g * gw + (q + 1) * LANES)
            xdt_q = xdt[:, lsl].astype(BF16)
            y_q = jnp.zeros((ln, LANES), F32)
            for hh in range(per):
                h = g * hpg + q * per + hh
                seg = acum[:, h:h + 1] - acum_t[h:h + 1, :]
                m = cb * jnp.exp(jnp.where(causal, seg, -jnp.inf))
                yd = jnp.dot(m.astype(BF16), xdt_q, preferred_element_type=F32)
                y_q = jnp.where(lane // p_dim == hh, yd, y_q)
            y_parts.append(y_q + y_off[:, q * LANES:(q + 1) * LANES])
        cs = jnp.dot(xdt_end[:, rows].T.astype(BF16), bg.astype(BF16), preferred_element_type=F32)
        for hh in range(hpg):
            h = g * hpg + hh
            hr = slice(h * p_dim, (h + 1) * p_dim)
            dec = jnp.broadcast_to(cdec[h:h + 1, :], (p_dim, n_state))
            st_ref[hr, :] = st_ref[hr, :] * dec + cs[hh * p_dim:(hh + 1) * p_dim, :]
    y = jnp.concatenate(y_parts, axis=1)
    y_ref[...] = _gated_group_norm(y, xs, z_ref[...], dskip_ref[...], gn_ref[...], n_groups)
    hl_ref[...] = st_ref[...]


def _ssd_prompt_call(xc, dt, z, alog_pad, dskip_x, gnorm, *, nhs, p_dim, n_state, n_groups):
    t, dconv = xc.shape
    dssd = nhs * p_dim
    ln = SSD_CHUNK
    assert t % ln == 0 and LANES % p_dim == 0 and (nhs // n_groups) % (LANES // p_dim) == 0
    const = lambda i: (0, 0)
    kern = functools.partial(_ssd_prompt_kernel, nhs=nhs, p_dim=p_dim, n_state=n_state,
                             n_groups=n_groups)
    return pl.pallas_call(
        kern,
        grid=(t // ln,),
        in_specs=[pl.BlockSpec((ln, dconv), lambda i: (i, 0)),
                  pl.BlockSpec((ln, LANES), lambda i: (i, 0)),
                  pl.BlockSpec((ln, dssd), lambda i: (i, 0)),
                  pl.BlockSpec((1, LANES), const), pl.BlockSpec((1, dssd), const),
                  pl.BlockSpec((1, dssd), const)],
        out_specs=[pl.BlockSpec((ln, dssd), lambda i: (i, 0)),
                   pl.BlockSpec((dssd, n_state), const)],
        out_shape=[jax.ShapeDtypeStruct((t, dssd), BF16), jax.ShapeDtypeStruct((dssd, n_state), F32)],
        scratch_shapes=[pltpu.VMEM((dssd, n_state), F32)],
        compiler_params=_cparams("arbitrary"),
        name="ssd_prompt",
    )(xc, dt, z, alog_pad, dskip_x, gnorm)


def _ssd_sample_kernel(xc_ref, dt_ref, z_ref, alog_ref, dskip_ref, gn_ref, st_ref, y_ref, so_ref,
                       *, nhs, p_dim, n_state, n_groups):
    bb = xc_ref.shape[0]
    dssd = nhs * p_dim
    gw = dssd // n_groups
    xs = xc_ref[:, 0:dssd]
    dt = dt_ref[...]
    a = -jnp.exp(alog_ref[...])
    expand = _expand_matrix(LANES, nhs, p_dim)
    dec_x = _xdot_l(jnp.exp(dt * a), expand)
    xdt = xs * _xdot_l(dt, expand)

    def column(row):
        return jnp.broadcast_to(row, (n_state, dssd)).T

    ys = []
    for r in range(bb):
        bfull = jnp.concatenate(
            [jnp.broadcast_to(xc_ref[r:r + 1, dssd + g * n_state:dssd + (g + 1) * n_state],
                              (gw, n_state)) for g in range(n_groups)], axis=0)
        cfull = jnp.concatenate(
            [jnp.broadcast_to(xc_ref[r:r + 1, dssd + (n_groups + g) * n_state:
                                     dssd + (n_groups + g + 1) * n_state],
                              (gw, n_state)) for g in range(n_groups)], axis=0)
        s_new = st_ref[r] * column(dec_x[r:r + 1, :]) + column(xdt[r:r + 1, :]) * bfull
        so_ref[r] = s_new
        ycol = jnp.sum(s_new * cfull, axis=1, keepdims=True)
        ys.append(jnp.broadcast_to(ycol, (dssd, n_state)).T[0:1, :])
    y = jnp.concatenate(ys, axis=0)
    y_ref[...] = _gated_group_norm(y, xs, z_ref[...], dskip_ref[...], gn_ref[...], n_groups)


def _ssd_sample_call(xc, dt, z, alog_pad, dskip_x, gnorm, state, *, nhs, p_dim, n_state, n_groups):
    db, dconv = xc.shape
    dssd = nhs * p_dim
    bb = SUBLANES
    assert db % bb == 0 and n_state == LANES
    const = lambda i: (0, 0)
    kern = functools.partial(_ssd_sample_kernel, nhs=nhs, p_dim=p_dim, n_state=n_state,
                             n_groups=n_groups)
    return pl.pallas_call(
        kern,
        grid=(db // bb,),
        in_specs=[pl.BlockSpec((bb, dconv), lambda i: (i, 0)),
                  pl.BlockSpec((bb, LANES), lambda i: (i, 0)),
                  pl.BlockSpec((bb, dssd), lambda i: (i, 0)),
                  pl.BlockSpec((1, LANES), const), pl.BlockSpec((1, dssd), const),
                  pl.BlockSpec((1, dssd), const),
                  pl.BlockSpec((bb, dssd, n_state), lambda i: (i, 0, 0))],
        out_specs=[pl.BlockSpec((bb, dssd), lambda i: (i, 0)),
                   pl.BlockSpec((bb, dssd, n_state), lambda i: (i, 0, 0))],
        out_shape=[jax.ShapeDtypeStruct((db, dssd), BF16),
                   jax.ShapeDtypeStruct((db, dssd, n_state), F32)],
        compiler_params=_cparams("parallel"),
        name="ssd_sample",
    )(xc, dt, z, alog_pad, dskip_x, gnorm, state)


def _pad_lanes(v):
    return jnp.zeros((1, LANES), F32).at[0, :v.shape[0]].set(v)


def kernel(x_prompt, x_sample, cache_k, cache_v, cache_logf, state_ssm, state_conv, page_table, c_prompt, c_sample, w_ada, b_ada, norm_ffn1, w_ffn1_in, w_ffn1_out, norm_mix, w_in, b_forget, conv_w, conv_b, dt_bias, a_log, d_skip, ssd_norm, w_out, norm_ffn2, w_ffn2_in, w_ffn2_out, norm_final):
    bsz, t, d = x_prompt.shape
    db, s_new, _ = x_sample.shape
    depth, n_phys, page, nh, hd = cache_k.shape
    _, _, nhs, p_dim, n_state = state_ssm.shape
    conv_w1, dconv = state_conv.shape[2:]
    datt, dssd = nh * hd, nhs * p_dim
    n_groups = (dconv - dssd) // (2 * n_state)
    assert bsz == 1 and s_new == 1 and db % SUBLANES == 0 and c_prompt.shape[0] == 1
    scale = hd ** -0.5
    prompt_row_block = db // SUBLANES
    tm = min(512, t)
    n_pages = page_table.shape[1]
    pg = min(16, n_pages)

    xp = x_prompt.reshape(t, d)
    xs_ = x_sample.reshape(db, d)
    ckt = jnp.transpose(cache_k, (0, 1, 3, 4, 2)).reshape(depth * n_phys, datt, page)
    cvt = jnp.transpose(cache_v, (0, 1, 3, 4, 2)).reshape(depth * n_phys, datt, page)
    clft = jnp.transpose(cache_logf, (0, 1, 3, 2)).reshape(depth * n_phys, nh, page)
    heads_last = lambda at, n: jnp.transpose(at.reshape(nh, hd, n), (2, 0, 1))
    c_all = jnp.concatenate([c_sample, c_prompt, jnp.zeros((SUBLANES - 1, d), F32)], axis=0)

    outs_p, outs_s = [], []
    for l in range(depth):
        sp = (datt, 2 * datt, 3 * datt, 3 * datt + nh, 3 * datt + nh + dssd,
              3 * datt + nh + dssd + dconv)
        wi = w_in[l]
        wqkv = wi[:, :sp[2]].astype(BF16)
        wzx = wi[:, sp[3]:sp[5]].astype(BF16)
        wfd = (jnp.zeros((d, 2 * LANES), F32).at[:, :nh].set(wi[:, sp[2]:sp[3]])
               .at[:, LANES:LANES + nhs].set(wi[:, sp[5]:])).astype(BF16)
        bfd = jnp.concatenate([_pad_lanes(b_forget[l]), _pad_lanes(dt_bias[l])], axis=1)
        alog_pad = _pad_lanes(a_log[l])
        dskip_x = jnp.repeat(d_skip[l], p_dim).reshape(1, dssd)
        gnorm = ssd_norm[l].reshape(1, dssd)
        w1i, w1o = w_ffn1_in[l].astype(BF16), w_ffn1_out[l].astype(BF16)
        w2i, w2o = w_ffn2_in[l].astype(BF16), w_ffn2_out[l].astype(BF16)
        wo = w_out[l].astype(BF16)
        last = l == depth - 1

        m_all = _ada_call(c_all, w_ada[l], b_ada[l])
        proj = dict(datt=datt, dssd=dssd, nh=nh, nhs=nhs, scale=scale)
        ssd = dict(nhs=nhs, p_dim=p_dim, n_state=n_state, n_groups=n_groups)

        xp = _ffn_call(xp, m_all, 0, norm_ffn1[l], w1i, w1o, per_row=False, tm=tm,
                       prompt_row_block=prompt_row_block, name="ffn1_prompt")
        (qa, ka, vbt, kt, vt, lft, z, xc, dt, tail) = _inproj_prompt_call(
            xp, m_all, prompt_row_block, norm_mix[l], wqkv, wzx, wfd, bfd, conv_w[l], conv_b[l],
            tm=tm, hd=hd, **proj)
        att = _attn_prompt_call(qa, ka, vbt, nh=nh, hd=hd, tq=tm)
        yssd, h_last = _ssd_prompt_call(xc, dt, z, alog_pad, dskip_x, gnorm, **ssd)
        xp = _ffn_call(xp, m_all, 6, norm_ffn2[l], w2i, w2o, per_row=False, tm=tm,
                       prompt_row_block=prompt_row_block, mix=(att, yssd, wo, 5),
                       final_norm_w=norm_final if last else None, name="ffn2_prompt")
        outs_p.append((heads_last(kt, t)[None], heads_last(vt, t)[None], lft.T[None],
                       h_last.reshape(1, nhs, p_dim, n_state),
                       tail[SUBLANES - conv_w1:].reshape(1, conv_w1, dconv)))

        xs_ = _ffn_call(xs_, m_all, 0, norm_ffn1[l], w1i, w1o, per_row=True, tm=db,
                        prompt_row_block=prompt_row_block, name="ffn1_sample")
        (q, k, v, kt, vt, logf, lft, z, xc, dt, conv_new) = _inproj_sample_call(
            xs_, m_all, norm_mix[l], wqkv, wzx, wfd, bfd, conv_w[l], conv_b[l],
            jnp.transpose(state_conv[l], (1, 0, 2)), **proj)
        att = _attn_sample_call(page_table, q, k, v, logf, ckt, cvt, clft, l * n_phys,
                                nh=nh, hd=hd, pg=pg)
        yssd, st_new = _ssd_sample_call(xc, dt, z, alog_pad, dskip_x, gnorm,
                                        state_ssm[l].reshape(db, dssd, n_state), **ssd)
        xs_ = _ffn_call(xs_, m_all, 6, norm_ffn2[l], w2i, w2o, per_row=True, tm=db,
                        prompt_row_block=prompt_row_block, mix=(att, yssd, wo, 5),
                        final_norm_w=norm_final if last else None, name="ffn2_sample")
        outs_s.append((heads_last(kt, db)[:, None], heads_last(vt, db)[:, None], lft.T[:, None],
                       st_new.reshape(db, nhs, p_dim, n_state), jnp.transpose(conv_new, (1, 0, 2))))

    stack = lambda outs, i: jnp.stack([o[i] for o in outs])
    return (xp.reshape(bsz, t, d), xs_.reshape(db, 1, d),
            stack(outs_p, 0), stack(outs_p, 1), stack(outs_p, 2), stack(outs_p, 3), stack(outs_p, 4),
            stack(outs_s, 0), stack(outs_s, 1), stack(outs_s, 2), stack(outs_s, 3), stack(outs_s, 4))
```

```python
import functools

import math

import jax
import jax.numpy as jnp
import numpy as np
from jax import lax
from jax.experimental import pallas as pl
from jax.experimental.pallas import tpu as pltpu

F32 = jnp.float32
BF16 = jnp.bfloat16
EPS = 1e-6
LOG2E = math.log2(math.e)
LANES = 128
SUBLANES = 8
SSD_CHUNK = 128
VMEM_LIMIT = 56 * 1024 * 1024


def _cparams(*semantics):
    return pltpu.CompilerParams(dimension_semantics=semantics, vmem_limit_bytes=VMEM_LIMIT)


def _dot(a, b):
    return jnp.dot(a.astype(BF16), b.astype(BF16), preferred_element_type=F32)


def _dot_nt(a, b):
    return lax.dot_general(a.astype(BF16), b.astype(BF16), (((1,), (1,)), ((), ())),
                           preferred_element_type=F32)


def _split3(a):
    hi = a.astype(BF16)
    r = a - hi.astype(F32)
    mid = r.astype(BF16)
    lo = (r - mid.astype(F32)).astype(BF16)
    return hi, mid, lo


def _xdot_l(a, e):
    hi, mid, lo = _split3(a)
    f = functools.partial(jnp.dot, preferred_element_type=F32)
    return f(hi, e) + f(mid, e) + f(lo, e)


def _xdot_r(e, b):
    hi, mid, lo = _split3(b)
    f = functools.partial(jnp.dot, preferred_element_type=F32)
    return f(e, hi) + f(e, mid) + f(e, lo)


def _silu(x):
    return x * jax.nn.sigmoid(x)


def _softplus(x):
    return jnp.maximum(x, 0.0) + jnp.log1p(jnp.exp(-jnp.abs(x)))


def _rms(x):
    return x * lax.rsqrt(jnp.mean(x * x, axis=-1, keepdims=True) + EPS)


def _norm_mod(x, nw, shift, scale):
    return (_rms(x) * nw) * (1.0 + scale) + shift


def _expand_matrix(n_rows, n_heads, width):
    row = lax.broadcasted_iota(jnp.int32, (n_rows, n_heads * width), 0)
    col = lax.broadcasted_iota(jnp.int32, (n_rows, n_heads * width), 1)
    return (col // width == row).astype(BF16)


def _tri_matrix(n):
    row = lax.broadcasted_iota(jnp.int32, (n, n), 0)
    col = lax.broadcasted_iota(jnp.int32, (n, n), 1)
    return (col <= row).astype(BF16)


def _gated_group_norm(y, xs, z, dskip, gnorm, n_groups):
    y = (y + dskip * xs) * _silu(z)
    gw = y.shape[1] // n_groups
    parts = [_rms(y[:, g * gw:(g + 1) * gw]) for g in range(n_groups)]
    return (jnp.concatenate(parts, axis=1) * gnorm).astype(BF16)


def _ada_kernel(c_ref, w_ref, b_ref, o_ref):
    o_ref[...] = _dot(_silu(c_ref[...]), w_ref[...]) + b_ref[...]


def _ada_call(c_all, w, b):
    r, d = c_all.shape
    n = w.shape[1]
    tn = d
    return pl.pallas_call(
        _ada_kernel,
        grid=(n // tn,),
        in_specs=[pl.BlockSpec((r, d), lambda j: (0, 0)),
                  pl.BlockSpec((d, tn), lambda j: (0, j)),
                  pl.BlockSpec((1, tn), lambda j: (0, j))],
        out_specs=pl.BlockSpec((r, tn), lambda j: (0, j)),
        out_shape=jax.ShapeDtypeStruct((r, n), F32),
        compiler_params=_cparams("parallel"),
        name="ada",
    )(c_all, w, b.reshape(1, n))


def _mod_spec(per_row, tm, d, idx, prompt_row_block):
    if per_row:
        return pl.BlockSpec((tm, d), lambda i: (i, idx))
    return pl.BlockSpec((SUBLANES, d), lambda i: (prompt_row_block, idx))


def _ffn_kernel(*refs, per_row, with_mix, final_norm, ff, tf, datt):
    it = iter(refs)
    x_ref = next(it)
    if with_mix:
        att_ref, ys_ref, wo_ref, g2_ref = next(it), next(it), next(it), next(it)
    sh_ref, sc_ref, g_ref, nw_ref, win_ref, wout_ref = (next(it) for _ in range(6))
    nf_ref = next(it) if final_norm else None
    o_ref = next(it)
    a_ref = next(it)
    rows = slice(None) if per_row else slice(0, 1)

    x = x_ref[...]
    if with_mix:
        mix = (jnp.dot(att_ref[...].astype(BF16), wo_ref[0:datt, :], preferred_element_type=F32)
               + jnp.dot(ys_ref[...], wo_ref[datt:, :], preferred_element_type=F32))
        x = x + g2_ref[rows, :] * mix
    h = _norm_mod(x, nw_ref[...], sh_ref[rows, :], sc_ref[rows, :]).astype(BF16)
    for c in range(ff // tf):
        g = jnp.dot(h, win_ref[:, c * tf:(c + 1) * tf], preferred_element_type=F32)
        u = jnp.dot(h, win_ref[:, ff + c * tf:ff + (c + 1) * tf], preferred_element_type=F32)
        a_ref[:, c * tf:(c + 1) * tf] = (_silu(g) * u).astype(BF16)
    y = x + (0.5 * g_ref[rows, :]) * jnp.dot(a_ref[...], wout_ref[...], preferred_element_type=F32)
    if final_norm:
        y = _rms(y) * nf_ref[...]
    o_ref[...] = y


def _ffn_call(x, m_all, ada_base, norm_w, w_in_b, w_out_b, *, per_row, tm, prompt_row_block,
              mix=None, final_norm_w=None, name):
    r, d = x.shape
    ff = w_out_b.shape[0]
    tf = 256
    assert r % tm == 0 and ff % tf == 0
    const = lambda i: (0, 0)
    resident = functools.partial(pl.BlockSpec, index_map=const, pipeline_mode=pl.Buffered(1))
    mspec = functools.partial(_mod_spec, per_row, tm, d, prompt_row_block=prompt_row_block)
    args, specs = [x], [pl.BlockSpec((tm, d), lambda i: (i, 0))]
    datt = 0
    if mix is not None:
        att, ys, wo_b, g2_idx = mix
        datt = att.shape[1]
        args += [att, ys, wo_b, m_all]
        specs += [pl.BlockSpec((tm, datt), lambda i: (i, 0)),
                  pl.BlockSpec((tm, ys.shape[1]), lambda i: (i, 0)),
                  resident(wo_b.shape), mspec(g2_idx)]
    args += [m_all, m_all, m_all, norm_w.reshape(1, d), w_in_b, w_out_b]
    specs += [mspec(ada_base), mspec(ada_base + 1), mspec(ada_base + 2),
              pl.BlockSpec((1, d), const), resident(w_in_b.shape), resident(w_out_b.shape)]
    if final_norm_w is not None:
        args.append(final_norm_w.reshape(1, d))
        specs.append(pl.BlockSpec((1, d), const))
    kern = functools.partial(_ffn_kernel, per_row=per_row, with_mix=mix is not None,
                             final_norm=final_norm_w is not None, ff=ff, tf=tf, datt=datt)
    return pl.pallas_call(
        kern,
        grid=(r // tm,),
        in_specs=specs,
        out_specs=pl.BlockSpec((tm, d), lambda i: (i, 0)),
        out_shape=jax.ShapeDtypeStruct((r, d), F32),
        scratch_shapes=[pltpu.VMEM((tm, ff), BF16)],
        compiler_params=_cparams("parallel"),
        name=name,
    )(*args)


def _project(x, sh, sc, nw_ref, wqkv_ref, wzx_ref, wfd_ref, bfd_ref, *, datt, dssd, nh, nhs):
    h = _norm_mod(x, nw_ref[...], sh, sc).astype(BF16)
    qkv = jnp.dot(h, wqkv_ref[...], preferred_element_type=F32)
    zx = jnp.dot(h, wzx_ref[...], preferred_element_type=F32)
    fd = jnp.dot(h, wfd_ref[...], preferred_element_type=F32) + bfd_ref[...]
    lane = lax.broadcasted_iota(jnp.int32, (1, LANES), 1)
    logf = jnp.where(lane < nh, -_softplus(-fd[:, :LANES]), 0.0)
    dt = jnp.where(lane < nhs, _softplus(fd[:, LANES:]), 0.0)
    return (qkv[:, :datt], qkv[:, datt:2 * datt], qkv[:, 2 * datt:], zx[:, :dssd], zx[:, dssd:],
            logf, dt)


def _inproj_prompt_kernel(x_ref, sh_ref, sc_ref, nw_ref, wqkv_ref, wzx_ref, wfd_ref, bfd_ref,
                          cw_ref, cb_ref, selq_ref, qone_ref, selk_ref, kone_ref,
                          qa_ref, ka_ref, vbt_ref, kt_ref, vt_ref, lft_ref, z_ref,
                          xc_ref, dt_ref, tail_ref,
                          ptail_ref, cf_ref, *, datt, dssd, nh, nhs, hd, scale, conv_w):
    @pl.when(pl.program_id(0) == 0)
    def _():
        ptail_ref[...] = jnp.zeros_like(ptail_ref)
        cf_ref[...] = jnp.zeros_like(cf_ref)

    tm = x_ref.shape[0]
    q, k, v, z, xbc, logf, dt = _project(
        x_ref[...], sh_ref[0:1, :], sc_ref[0:1, :], nw_ref, wqkv_ref, wzx_ref, wfd_ref, bfd_ref,
        datt=datt, dssd=dssd, nh=nh, nhs=nhs)
    kt_ref[...] = k.T
    vt = v.T
    vt_ref[...] = vt
    for h in range(nh):
        vbt_ref[h] = vt[h * hd:(h + 1) * hd, :].astype(BF16)
    z_ref[...] = z
    lft_ref[...] = logf.T[0:SUBLANES, :]
    dt_ref[...] = dt

    fc = _xdot_r(_tri_matrix(tm), logf) + cf_ref[...]
    cf_ref[...] = fc[tm - 1:tm, :]

    per = LANES // hd
    lane = lax.broadcasted_iota(jnp.int32, (1, LANES), 1)
    f3 = jnp.concatenate(_split3(fc * LOG2E), axis=1)
    qs = q * (scale * LOG2E)
    qbias = (jnp.dot(f3, selq_ref[...], preferred_element_type=F32) + qone_ref[...]).astype(BF16)
    kbias = (jnp.dot(f3, selk_ref[...], preferred_element_type=F32) + kone_ref[...]).astype(BF16)
    for h in range(nh):
        g, hh = divmod(h, per)
        qpair = qs[:, g * LANES:(g + 1) * LANES]
        qa_ref[h, :, 0:LANES] = jnp.where(lane // hd == hh, qpair, 0.0).astype(BF16)
        qa_ref[h, :, LANES:2 * LANES] = qbias[:, h * LANES:(h + 1) * LANES]
    for g in range(nh // per):
        ka_ref[g, :, 0:LANES] = k[:, g * LANES:(g + 1) * LANES].astype(BF16)
        ka_ref[g, :, LANES:2 * LANES] = kbias[:, g * LANES:(g + 1) * LANES]

    ptail = ptail_ref[...]
    row8 = lax.broadcasted_iota(jnp.int32, ptail.shape, 0)
    acc = cb_ref[...]
    for i in range(conv_w):
        s = conv_w - 1 - i
        if s == 0:
            xs = xbc
        else:
            r = pltpu.roll(xbc, s, 0)
            top = jnp.where(row8 < s, pltpu.roll(ptail, s, 0), r[0:SUBLANES])
            xs = jnp.concatenate([top, r[SUBLANES:]], axis=0)
        acc = acc + xs * cw_ref[i:i + 1, :]
    xc_ref[...] = _silu(acc)
    tail = xbc[tm - SUBLANES:tm]
    ptail_ref[...] = tail
    tail_ref[...] = tail


def _bias_selectors(nh, hd):
    per = LANES // hd
    selq = np.zeros((3 * LANES, nh, LANES), np.float32)
    qone = np.zeros((nh, LANES), np.float32)
    selk = np.zeros((3 * LANES, nh // per, LANES), np.float32)
    kone = np.zeros((nh // per, LANES), np.float32)
    kone[:, 0:3] = 1.0
    for h in range(nh):
        g, hh = divmod(h, per)
        for c in range(3):
            selq[c * LANES + h, h, c] = 1.0
            selk[c * LANES + h, g, 3 + 3 * hh + c] = -1.0
        qone[h, 3 + 3 * hh:6 + 3 * hh] = 1.0
    return (jnp.asarray(selq.reshape(3 * LANES, -1), BF16), jnp.asarray(qone.reshape(1, -1), F32),
            jnp.asarray(selk.reshape(3 * LANES, -1), BF16), jnp.asarray(kone.reshape(1, -1), F32))


def _inproj_prompt_call(x, m_all, prompt_row_block, norm_w, wqkv, wzx, wfd, bfd, cw, cb, *,
                        datt, dssd, nh, nhs, hd, scale, tm):
    t, d = x.shape
    dconv = cw.shape[1]
    per = LANES // hd
    assert t % tm == 0 and nh == SUBLANES and conv_w_ok(cw.shape[0]) and 3 + 3 * per <= LANES
    const = lambda i: (0, 0)
    full = lambda a: pl.BlockSpec(a.shape, lambda i: (0,) * a.ndim)
    rowblk = lambda w: pl.BlockSpec((tm, w), lambda i: (i, 0))
    colblk = lambda h: pl.BlockSpec((h, tm), lambda i: (0, i))
    mspec = functools.partial(_mod_spec, False, tm, d, prompt_row_block=prompt_row_block)
    kern = functools.partial(_inproj_prompt_kernel, datt=datt, dssd=dssd, nh=nh, nhs=nhs, hd=hd,
                             scale=scale, conv_w=cw.shape[0])
    sel = _bias_selectors(nh, hd)
    sds = jax.ShapeDtypeStruct
    return pl.pallas_call(
        kern,
        grid=(t // tm,),
        in_specs=[rowblk(d), mspec(3), mspec(4), pl.BlockSpec((1, d), const),
                  full(wqkv), full(wzx), full(wfd), full(bfd), full(cw),
                  pl.BlockSpec((1, dconv), const)] + [full(a) for a in sel],
        out_specs=[pl.BlockSpec((nh, tm, 2 * LANES), lambda i: (0, i, 0)),
                   pl.BlockSpec((nh // per, tm, 2 * LANES), lambda i: (0, i, 0)),
                   pl.BlockSpec((nh, hd, tm), lambda i: (0, 0, i)),
                   colblk(datt), colblk(datt), colblk(SUBLANES),
                   rowblk(dssd), rowblk(dconv), rowblk(LANES),
                   pl.BlockSpec((SUBLANES, dconv), const)],
        out_shape=[sds((nh, t, 2 * LANES), BF16), sds((nh // per, t, 2 * LANES), BF16),
                   sds((nh, hd, t), BF16), sds((datt, t), F32), sds((datt, t), F32),
                   sds((SUBLANES, t), F32), sds((t, dssd), F32),
                   sds((t, dconv), F32), sds((t, LANES), F32), sds((SUBLANES, dconv), F32)],
        scratch_shapes=[pltpu.VMEM((SUBLANES, dconv), F32), pltpu.VMEM((1, LANES), F32)],
        compiler_params=_cparams("arbitrary"),
        name="inproj_prompt",
    )(x, m_all, m_all, norm_w.reshape(1, d), wqkv, wzx, wfd, bfd, cw, cb.reshape(1, dconv), *sel)


def conv_w_ok(w):
    return 1 <= w - 1 < SUBLANES


def _inproj_sample_kernel(x_ref, sh_ref, sc_ref, nw_ref, wqkv_ref, wzx_ref, wfd_ref, bfd_ref,
                          cw_ref, cb_ref, cs_ref,
                          q_ref, k_ref, v_ref, kt_ref, vt_ref, lf_ref, lft_ref, z_ref, xc_ref, dt_ref,
                          cn_ref, *, datt, dssd, nh, nhs, scale, conv_w):
    q, k, v, z, xbc, logf, dt = _project(
        x_ref[...], sh_ref[...], sc_ref[...], nw_ref, wqkv_ref, wzx_ref, wfd_ref, bfd_ref,
        datt=datt, dssd=dssd, nh=nh, nhs=nhs)
    q_ref[...] = q * scale
    k_ref[...] = k
    v_ref[...] = v
    kt_ref[...] = k.T
    vt_ref[...] = v.T
    z_ref[...] = z
    lf_ref[...] = logf[:, 0:nh]
    lft_ref[...] = logf.T[0:SUBLANES, :]
    dt_ref[...] = dt
    acc = cb_ref[...]
    for i in range(conv_w - 1):
        acc = acc + cs_ref[i] * cw_ref[i:i + 1, :]
    acc = acc + xbc * cw_ref[conv_w - 1:conv_w, :]
    xc_ref[...] = _silu(acc)
    for i in range(conv_w - 2):
        cn_ref[i] = cs_ref[i + 1]
    cn_ref[conv_w - 2] = xbc


def _inproj_sample_call(x, m_all, norm_w, wqkv, wzx, wfd, bfd, cw, cb, conv_state, *,
                        datt, dssd, nh, nhs, scale):
    r, d = x.shape
    conv_w, dconv = cw.shape
    const = lambda i: (0, 0)
    full = lambda a: pl.BlockSpec(a.shape, lambda i: (0,) * a.ndim)
    mspec = functools.partial(_mod_spec, True, r, d, prompt_row_block=0)
    kern = functools.partial(_inproj_sample_kernel, datt=datt, dssd=dssd, nh=nh, nhs=nhs,
                             scale=scale, conv_w=conv_w)
    sds = jax.ShapeDtypeStruct
    shapes = [sds((r, datt), F32), sds((r, datt), F32), sds((r, datt), F32), sds((datt, r), F32),
              sds((datt, r), F32), sds((r, nh), F32), sds((SUBLANES, r), F32), sds((r, dssd), F32),
              sds((r, dconv), F32), sds((r, LANES), F32), sds(conv_state.shape, F32)]
    return pl.pallas_call(
        kern,
        grid=(1,),
        in_specs=[full(x), mspec(3), mspec(4), pl.BlockSpec((1, d), const), full(wqkv), full(wzx),
                  full(wfd), full(bfd), full(cw), pl.BlockSpec((1, dconv), const), full(conv_state)],
        out_specs=[full(s) for s in shapes],
        out_shape=shapes,
        compiler_params=_cparams("arbitrary"),
        name="inproj_sample",
    )(x, m_all, m_all, norm_w.reshape(1, d), wqkv, wzx, wfd, bfd, cw, cb.reshape(1, dconv), conv_state)


def _attn_prompt_kernel(qi_ref, kj_ref, q_ref, k_ref, vt_ref, o_ref, m_ref, l_ref, acc_ref, *, nh, hd):
    step_id = pl.program_id(0)
    i = qi_ref[step_id]
    j = kj_ref[step_id]
    tq, tk = q_ref.shape[1], k_ref.shape[1]
    kpq = tq // tk
    per = LANES // hd

    @pl.when(j == 0)
    def _():
        m_ref[...] = jnp.full_like(m_ref, -jnp.inf)
        l_ref[...] = jnp.zeros_like(l_ref)
        acc_ref[...] = jnp.zeros_like(acc_ref)

    ones = jnp.ones((2 * SUBLANES, tk), BF16)

    def step(masked):
        if masked:
            keep = (lax.broadcasted_iota(jnp.int32, (tk, tq), 0)
                    - lax.broadcasted_iota(jnp.int32, (tk, tq), 1)) <= i * tq - j * tk

        def scores(h):
            return lax.dot_general(k_ref[h // per], q_ref[h], (((1,), (1,)), ((), ())),
                                   preferred_element_type=F32)

        def softmax(h, st):
            if masked:
                st = jnp.where(keep, st, -jnp.inf)
            m_old = m_ref[h]
            m_new = jnp.maximum(m_old, jnp.max(st, axis=0, keepdims=True))
            m_ref[h] = m_new
            return jnp.exp2(st - m_new).astype(BF16), jnp.exp2(m_old - m_new)

        def values(h, p, alpha):
            pv = jnp.dot(jnp.concatenate([vt_ref[h], ones], axis=0), p,
                         preferred_element_type=F32)
            acc_ref[h] = alpha * acc_ref[h] + pv[0:hd]
            l_ref[h] = alpha * l_ref[h] + pv[hd:hd + 1]

        st = {0: scores(0)}
        pa = {}
        for h in range(nh + 1):
            if h + 1 < nh:
                st[h + 1] = scores(h + 1)
            if h < nh:
                pa[h] = softmax(h, st.pop(h))
            if h >= 1:
                values(h - 1, *pa.pop(h - 1))

    @pl.when(j < kpq * i)
    def _():
        step(False)

    @pl.when(j >= kpq * i)
    def _():
        step(True)

    @pl.when(j == kpq * (i + 1) - 1)
    def _():
        out_t = jnp.concatenate([acc_ref[h] * (1.0 / l_ref[h]) for h in range(nh)], axis=0)
        o_ref[...] = out_t.T.astype(BF16)


def _attn_prompt_call(qa, ka, vbt, *, nh, hd, tq, tk):
    t = qa.shape[1]
    datt = nh * hd
    per = LANES // hd
    assert t % tq == 0 and tq % tk == 0 and LANES % hd == 0 and nh % per == 0
    nq, kpq = t // tq, tq // tk
    qi = np.concatenate([np.full(kpq * (i + 1), i, np.int32) for i in range(nq)])
    kj = np.concatenate([np.arange(kpq * (i + 1), dtype=np.int32) for i in range(nq)])
    kern = functools.partial(_attn_prompt_kernel, nh=nh, hd=hd)
    return pl.pallas_call(
        kern,
        grid_spec=pltpu.PrefetchScalarGridSpec(
            num_scalar_prefetch=2,
            grid=(len(qi),),
            in_specs=[pl.BlockSpec((nh, tq, qa.shape[2]), lambda s, qi, kj: (0, qi[s], 0)),
                      pl.BlockSpec((nh // per, tk, ka.shape[2]), lambda s, qi, kj: (0, kj[s], 0)),
                      pl.BlockSpec((nh, hd, tk), lambda s, qi, kj: (0, 0, kj[s]))],
            out_specs=pl.BlockSpec((tq, datt), lambda s, qi, kj: (qi[s], 0)),
            scratch_shapes=[pltpu.VMEM((nh, 1, tq), F32), pltpu.VMEM((nh, 1, tq), F32),
                            pltpu.VMEM((nh, hd, tq), F32)]),
        out_shape=jax.ShapeDtypeStruct((t, datt), BF16),
        compiler_params=_cparams("arbitrary"),
        name="attn_prompt",
    )(jnp.asarray(qi), jnp.asarray(kj), qa, ka, vbt)


def _attn_sample_kernel(pt_ref, q_ref, kn_ref, vn_ref, lfn_ref, *rest, nh, hd, pg, sub):
    k_refs, v_refs, lf_refs = rest[0:pg], rest[pg:2 * pg], rest[2 * pg:3 * pg]
    o_ref, qcol_ref, m_ref, l_ref, acc_ref, cf_ref = rest[3 * pg:]
    j = pl.program_id(1)
    datt, page = k_refs[0].shape

    def head_rows(a):
        return jnp.concatenate([jnp.broadcast_to(a[h:h + 1, :], (hd, a.shape[1])) for h in range(nh)],
                               axis=0)

    @pl.when(j == 0)
    def _():
        qcol_ref[...] = jnp.broadcast_to(q_ref[...], (page, datt)).T
        m_ref[...] = jnp.full_like(m_ref, -jnp.inf)
        l_ref[...] = jnp.zeros_like(l_ref)
        acc_ref[...] = jnp.zeros_like(acc_ref)
        cf_ref[...] = jnp.zeros_like(cf_ref)

    triu = (lax.broadcasted_iota(jnp.int32, (page, page), 0)
            <= lax.broadcasted_iota(jnp.int32, (page, page), 1)).astype(BF16)
    f_loc = _xdot_l(jnp.concatenate([r[...] for r in lf_refs], axis=0), triu)
    f_tot = jnp.broadcast_to(f_loc[:, page - 1:page], f_loc.shape)
    carry = [cf_ref[...]]

    def scores(pages):
        out = []
        for p in pages:
            s = jnp.concatenate(
                [jnp.sum(k_refs[p][h * hd:(h + 1) * hd, :] * qcol_ref[h * hd:(h + 1) * hd, :],
                         axis=0, keepdims=True) for h in range(nh)], axis=0)
            out.append(s - (f_loc[p * nh:(p + 1) * nh, :] + carry[0]))
            carry[0] = carry[0] + f_tot[p * nh:(p + 1) * nh, :]
        return out

    groups = [range(g, min(g + sub, pg)) for g in range(0, pg, sub)]
    m_new, l_new = m_ref[...], l_ref[...]
    sc_next = scores(groups[0])
    for gi, pages in enumerate(groups):
        sc = sc_next
        m_old = m_new
        for s in sc:
            m_new = jnp.maximum(m_new, jnp.max(s, axis=1, keepdims=True))
        alpha = jnp.exp(m_old - m_new)
        l_new = alpha * l_new
        probs = []
        for s in sc:
            pr = jnp.exp(s - m_new)
            l_new = l_new + jnp.sum(pr, axis=1, keepdims=True)
            probs.append(pr)
        if gi + 1 < len(groups):
            sc_next = scores(groups[gi + 1])
        for h in range(nh):
            hr = slice(h * hd, (h + 1) * hd)
            a = acc_ref[hr, :] * alpha[h:h + 1, :]
            for p, pr in zip(pages, probs):
                a = a + v_refs[p][hr, :] * pr[h:h + 1, :]
            acc_ref[hr, :] = a
    cf = carry[0]
    cf_ref[...] = cf
    m_ref[...] = m_new
    l_ref[...] = l_new

    @pl.when(j == pl.num_programs(1) - 1)
    def _():
        row = lax.broadcasted_iota(jnp.int32, (nh, datt), 0)
        col = lax.broadcasted_iota(jnp.int32, (nh, datt), 1)
        qk = jnp.where(col // hd == row, jnp.broadcast_to(q_ref[...] * kn_ref[...], (nh, datt)), 0.0)
        s_new = jnp.sum(qk, axis=1, keepdims=True) - (cf[:, 0:1] + lfn_ref[...])
        m_fin = jnp.maximum(m_new, s_new)
        a_fin = jnp.exp(m_new - m_fin)
        p_new = jnp.exp(s_new - m_fin)
        l_fin = a_fin * l_new + p_new
        tot = jnp.sum(acc_ref[...], axis=1, keepdims=True)
        vcol = jnp.broadcast_to(vn_ref[...], (LANES, datt)).T[:, 0:1]
        out = (tot * head_rows(a_fin) + head_rows(p_new.astype(BF16).astype(F32)) * vcol) \
            / head_rows(l_fin)
        o_ref[...] = jnp.broadcast_to(out, (datt, LANES)).T[0:1, :]


def _attn_sample_call(page_table, q, kn, vn, lfn, ckt, cvt, clft, layer_base, *, nh, hd, pg):
    db, datt = q.shape
    n_pages = page_table.shape[1]
    page = ckt.shape[2]
    assert n_pages % pg == 0
    row3 = lambda a: a.reshape(db, 1, a.shape[1])
    rowspec = lambda w: pl.BlockSpec((None, 1, w), lambda b, j, pt: (b, 0, 0))

    def page_spec(p, rows):
        return pl.BlockSpec((None, rows, page), lambda b, j, pt: (layer_base + pt[b, j * pg + p], 0, 0))

    kern = functools.partial(_attn_sample_kernel, nh=nh, hd=hd, pg=pg, sub=pg)
    out = pl.pallas_call(
        kern,
        grid_spec=pltpu.PrefetchScalarGridSpec(
            num_scalar_prefetch=1,
            grid=(db, n_pages // pg),
            in_specs=([rowspec(datt), rowspec(datt), rowspec(datt),
                       pl.BlockSpec((None, nh, 1), lambda b, j, pt: (b, 0, 0))]
                      + [page_spec(p, datt) for p in range(pg)]
                      + [page_spec(p, datt) for p in range(pg)]
                      + [page_spec(p, nh) for p in range(pg)]),
            out_specs=rowspec(datt),
            scratch_shapes=[pltpu.VMEM((datt, page), F32), pltpu.VMEM((nh, 1), F32),
                            pltpu.VMEM((nh, 1), F32), pltpu.VMEM((datt, page), F32),
                            pltpu.VMEM((nh, page), F32)]),
        out_shape=jax.ShapeDtypeStruct((db, 1, datt), F32),
        compiler_params=_cparams("parallel", "arbitrary"),
        name="attn_sample",
    )(page_table, row3(q), row3(kn), row3(vn), lfn.reshape(db, nh, 1),
      *([ckt] * pg), *([cvt] * pg), *([clft] * pg))
    return out.reshape(db, datt)


def _ssd_prompt_kernel(xc_ref, dt_ref, z_ref, alog_ref, dskip_ref, gn_ref, y_ref, hl_ref, st_ref,
                       *, nhs, p_dim, n_state, n_groups):
    @pl.when(pl.program_id(0) == 0)
    def _():
        st_ref[...] = jnp.zeros_like(st_ref)

    for c in range(xc_ref.shape[0] // SSD_CHUNK):
        _ssd_chunk(xc_ref, dt_ref, z_ref, alog_ref, dskip_ref, gn_ref, y_ref, st_ref,
                   slice(c * SSD_CHUNK, (c + 1) * SSD_CHUNK), nhs=nhs, p_dim=p_dim,
                   n_state=n_state, n_groups=n_groups)
    hl_ref[...] = st_ref[...]


def _ssd_chunk(xc_ref, dt_ref, z_ref, alog_ref, dskip_ref, gn_ref, y_ref, st_ref, rs,
               *, nhs, p_dim, n_state, n_groups):
    ln = rs.stop - rs.start
    dssd = nhs * p_dim
    hpg = nhs // n_groups
    per = LANES // p_dim
    xs = xc_ref[rs, 0:dssd]
    dt = dt_ref[rs, :]
    a = -jnp.exp(alog_ref[...])
    expand = _expand_matrix(LANES, nhs, p_dim)
    acum = _xdot_r(_tri_matrix(ln), dt * a)
    acum_t = acum.T
    acum_x = _xdot_l(acum, expand)
    xdt = xs * _xdot_l(dt, expand)
    a_last_x = acum_x[ln - 1:ln, :]
    xdt_end = xdt * jnp.exp(a_last_x - acum_x)
    cdec = jnp.exp(acum_t[:, ln - 1:ln])
    causal = (lax.broadcasted_iota(jnp.int32, (ln, ln), 1)
              <= lax.broadcasted_iota(jnp.int32, (ln, ln), 0))
    lane = lax.broadcasted_iota(jnp.int32, (1, LANES), 1)

    y_parts = []
    for g in range(n_groups):
        bg = xc_ref[rs, dssd + g * n_state:dssd + (g + 1) * n_state]
        cg = xc_ref[rs, dssd + (n_groups + g) * n_state:dssd + (n_groups + g + 1) * n_state]
        gw = hpg * p_dim
        rows = slice(g * gw, (g + 1) * gw)
        st_g = st_ref[rows, :]
        cb = _dot_nt(cg, bg)
        y_off = _dot_nt(cg, st_g) * jnp.exp(acum_x[:, rows])
        for q in range(hpg // per):
            lsl = slice(g * gw + q * LANES, g * gw + (q + 1) * LANES)
            xdt_q = xdt[:, lsl].astype(BF16)
            y_q = jnp.zeros((ln, LANES), F32)
            for hh in range(per):
                h = g * hpg + q * per + hh
                seg = acum[:, h:h + 1] - acum_t[h:h + 1, :]
                m = cb * jnp.exp(jnp.where(causal, seg, -jnp.inf))
                yd = jnp.dot(m.astype(BF16), xdt_q, preferred_element_type=F32)
                y_q = jnp.where(lane // p_dim == hh, yd, y_q)
            y_parts.append(y_q + y_off[:, q * LANES:(q + 1) * LANES])
        cs = jnp.dot(xdt_end[:, rows].T.astype(BF16), bg.astype(BF16), preferred_element_type=F32)
        for hh in range(hpg):
            h = g * hpg + hh
            hr = slice(h * p_dim, (h + 1) * p_dim)
            dec = jnp.broadcast_to(cdec[h:h + 1, :], (p_dim, n_state))
            st_ref[hr, :] = st_ref[hr, :] * dec + cs[hh * p_dim:(hh + 1) * p_dim, :]
    y = jnp.concatenate(y_parts, axis=1)
    y_ref[rs, :] = _gated_group_norm(y, xs, z_ref[rs, :], dskip_ref[...], gn_ref[...], n_groups)


def _ssd_prompt_call(xc, dt, z, alog_pad, dskip_x, gnorm, *, nhs, p_dim, n_state, n_groups):
    t, dconv = xc.shape
    dssd = nhs * p_dim
    ln = 2 * SSD_CHUNK if t % (2 * SSD_CHUNK) == 0 else SSD_CHUNK
    assert t % ln == 0 and LANES % p_dim == 0 and (nhs // n_groups) % (LANES // p_dim) == 0
    const = lambda i: (0, 0)
    kern = functools.partial(_ssd_prompt_kernel, nhs=nhs, p_dim=p_dim, n_state=n_state,
                             n_groups=n_groups)
    return pl.pallas_call(
        kern,
        grid=(t // ln,),
        in_specs=[pl.BlockSpec((ln, dconv), lambda i: (i, 0)),
                  pl.BlockSpec((ln, LANES), lambda i: (i, 0)),
                  pl.BlockSpec((ln, dssd), lambda i: (i, 0)),
                  pl.BlockSpec((1, LANES), const), pl.BlockSpec((1, dssd), const),
                  pl.BlockSpec((1, dssd), const)],
        out_specs=[pl.BlockSpec((ln, dssd), lambda i: (i, 0)),
                   pl.BlockSpec((dssd, n_state), const)],
        out_shape=[jax.ShapeDtypeStruct((t, dssd), BF16), jax.ShapeDtypeStruct((dssd, n_state), F32)],
        scratch_shapes=[pltpu.VMEM((dssd, n_state), F32)],
        compiler_params=_cparams("arbitrary"),
        name="ssd_prompt",
    )(xc, dt, z, alog_pad, dskip_x, gnorm)


def _ssd_sample_kernel(xc_ref, dt_ref, z_ref, alog_ref, dskip_ref, gn_ref, st_ref, y_ref, so_ref,
                       *, nhs, p_dim, n_state, n_groups):
    bb = xc_ref.shape[0]
    dssd = nhs * p_dim
    gw = dssd // n_groups
    xs = xc_ref[:, 0:dssd]
    dt = dt_ref[...]
    a = -jnp.exp(alog_ref[...])
    expand = _expand_matrix(LANES, nhs, p_dim)
    dec_x = _xdot_l(jnp.exp(dt * a), expand)
    xdt = xs * _xdot_l(dt, expand)

    def column(row):
        return jnp.broadcast_to(row, (n_state, dssd)).T

    ys = []
    for r in range(bb):
        bfull = jnp.concatenate(
            [jnp.broadcast_to(xc_ref[r:r + 1, dssd + g * n_state:dssd + (g + 1) * n_state],
                              (gw, n_state)) for g in range(n_groups)], axis=0)
        cfull = jnp.concatenate(
            [jnp.broadcast_to(xc_ref[r:r + 1, dssd + (n_groups + g) * n_state:
                                     dssd + (n_groups + g + 1) * n_state],
                              (gw, n_state)) for g in range(n_groups)], axis=0)
        s_new = st_ref[r] * column(dec_x[r:r + 1, :]) + column(xdt[r:r + 1, :]) * bfull
        so_ref[r] = s_new
        ycol = jnp.sum(s_new * cfull, axis=1, keepdims=True)
        ys.append(jnp.broadcast_to(ycol, (dssd, n_state)).T[0:1, :])
    y = jnp.concatenate(ys, axis=0)
    y_ref[...] = _gated_group_norm(y, xs, z_ref[...], dskip_ref[...], gn_ref[...], n_groups)


def _ssd_sample_call(xc, dt, z, alog_pad, dskip_x, gnorm, state, *, nhs, p_dim, n_state, n_groups):
    db, dconv = xc.shape
    dssd = nhs * p_dim
    bb = SUBLANES
    assert db % bb == 0 and n_state == LANES
    const = lambda i: (0, 0)
    kern = functools.partial(_ssd_sample_kernel, nhs=nhs, p_dim=p_dim, n_state=n_state,
                             n_groups=n_groups)
    return pl.pallas_call(
        kern,
        grid=(db // bb,),
        in_specs=[pl.BlockSpec((bb, dconv), lambda i: (i, 0)),
                  pl.BlockSpec((bb, LANES), lambda i: (i, 0)),
                  pl.BlockSpec((bb, dssd), lambda i: (i, 0)),
                  pl.BlockSpec((1, LANES), const), pl.BlockSpec((1, dssd), const),
                  pl.BlockSpec((1, dssd), const),
                  pl.BlockSpec((bb, dssd, n_state), lambda i: (i, 0, 0))],
        out_specs=[pl.BlockSpec((bb, dssd), lambda i: (i, 0)),
                   pl.BlockSpec((bb, dssd, n_state), lambda i: (i, 0, 0))],
        out_shape=[jax.ShapeDtypeStruct((db, dssd), BF16),
                   jax.ShapeDtypeStruct((db, dssd, n_state), F32)],
        compiler_params=_cparams("parallel"),
        name="ssd_sample",
    )(xc, dt, z, alog_pad, dskip_x, gnorm, state)


def _pad_lanes(v):
    return jnp.zeros((1, LANES), F32).at[0, :v.shape[0]].set(v)


def kernel(x_prompt, x_sample, cache_k, cache_v, cache_logf, state_ssm, state_conv, page_table, c_prompt, c_sample, w_ada, b_ada, norm_ffn1, w_ffn1_in, w_ffn1_out, norm_mix, w_in, b_forget, conv_w, conv_b, dt_bias, a_log, d_skip, ssd_norm, w_out, norm_ffn2, w_ffn2_in, w_ffn2_out, norm_final):
    bsz, t, d = x_prompt.shape
    db, s_new, _ = x_sample.shape
    depth, n_phys, page, nh, hd = cache_k.shape
    _, _, nhs, p_dim, n_state = state_ssm.shape
    conv_w1, dconv = state_conv.shape[2:]
    datt, dssd = nh * hd, nhs * p_dim
    n_groups = (dconv - dssd) // (2 * n_state)
    assert bsz == 1 and s_new == 1 and db % SUBLANES == 0 and c_prompt.shape[0] == 1
    scale = hd ** -0.5
    prompt_row_block = db // SUBLANES
    tm = min(512, t)
    n_pages = page_table.shape[1]
    pg = min(16, n_pages)

    xp = x_prompt.reshape(t, d)
    xs_ = x_sample.reshape(db, d)
    ckt = jnp.transpose(cache_k, (0, 1, 3, 4, 2)).reshape(depth * n_phys, datt, page)
    cvt = jnp.transpose(cache_v, (0, 1, 3, 4, 2)).reshape(depth * n_phys, datt, page)
    clft = jnp.transpose(cache_logf, (0, 1, 3, 2)).reshape(depth * n_phys, nh, page)
    heads_last = lambda at, n: jnp.transpose(at.reshape(nh, hd, n), (2, 0, 1))
    c_all = jnp.concatenate([c_sample, c_prompt, jnp.zeros((SUBLANES - 1, d), F32)], axis=0)

    outs_p, outs_s = [], []
    for l in range(depth):
        sp = (datt, 2 * datt, 3 * datt, 3 * datt + nh, 3 * datt + nh + dssd,
              3 * datt + nh + dssd + dconv)
        wi = w_in[l]
        wqkv = wi[:, :sp[2]].astype(BF16)
        wzx = wi[:, sp[3]:sp[5]].astype(BF16)
        wfd = (jnp.zeros((d, 2 * LANES), F32).at[:, :nh].set(wi[:, sp[2]:sp[3]])
               .at[:, LANES:LANES + nhs].set(wi[:, sp[5]:])).astype(BF16)
        bfd = jnp.concatenate([_pad_lanes(b_forget[l]), _pad_lanes(dt_bias[l])], axis=1)
        alog_pad = _pad_lanes(a_log[l])
        dskip_x = jnp.repeat(d_skip[l], p_dim).reshape(1, dssd)
        gnorm = ssd_norm[l].reshape(1, dssd)
        w1i, w1o = w_ffn1_in[l].astype(BF16), w_ffn1_out[l].astype(BF16)
        w2i, w2o = w_ffn2_in[l].astype(BF16), w_ffn2_out[l].astype(BF16)
        wo = w_out[l].astype(BF16)
        last = l == depth - 1

        m_all = _ada_call(c_all, w_ada[l], b_ada[l])
        proj = dict(datt=datt, dssd=dssd, nh=nh, nhs=nhs, scale=scale)
        ssd = dict(nhs=nhs, p_dim=p_dim, n_state=n_state, n_groups=n_groups)

        xp = _ffn_call(xp, m_all, 0, norm_ffn1[l], w1i, w1o, per_row=False, tm=tm,
                       prompt_row_block=prompt_row_block, name="ffn1_prompt")
        (qa, ka, vbt, kt, vt, lft, z, xc, dt, tail) = _inproj_prompt_call(
            xp, m_all, prompt_row_block, norm_mix[l], wqkv, wzx, wfd, bfd, conv_w[l], conv_b[l],
            tm=tm, hd=hd, **proj)
        att = _attn_prompt_call(qa, ka, vbt, nh=nh, hd=hd, tq=min(2 * tm, t), tk=tm)
        yssd, h_last = _ssd_prompt_call(xc, dt, z, alog_pad, dskip_x, gnorm, **ssd)
        xp = _ffn_call(xp, m_all, 6, norm_ffn2[l], w2i, w2o, per_row=False, tm=tm,
                       prompt_row_block=prompt_row_block, mix=(att, yssd, wo, 5),
                       final_norm_w=norm_final if last else None, name="ffn2_prompt")
        outs_p.append((heads_last(kt, t)[None], heads_last(vt, t)[None], lft.T[None],
                       h_last.reshape(1, nhs, p_dim, n_state),
                       tail[SUBLANES - conv_w1:].reshape(1, conv_w1, dconv)))

        xs_ = _ffn_call(xs_, m_all, 0, norm_ffn1[l], w1i, w1o, per_row=True, tm=db,
                        prompt_row_block=prompt_row_block, name="ffn1_sample")
        (q, k, v, kt, vt, logf, lft, z, xc, dt, conv_new) = _inproj_sample_call(
            xs_, m_all, norm_mix[l], wqkv, wzx, wfd, bfd, conv_w[l], conv_b[l],
            jnp.transpose(state_conv[l], (1, 0, 2)), **proj)
        att = _attn_sample_call(page_table, q, k, v, logf, ckt, cvt, clft, l * n_phys,
                                nh=nh, hd=hd, pg=pg)
        yssd, st_new = _ssd_sample_call(xc, dt, z, alog_pad, dskip_x, gnorm,
                                        state_ssm[l].reshape(db, dssd, n_state), **ssd)
        xs_ = _ffn_call(xs_, m_all, 6, norm_ffn2[l], w2i, w2o, per_row=True, tm=db,
                        prompt_row_block=prompt_row_block, mix=(att, yssd, wo, 5),
                        final_norm_w=norm_final if last else None, name="ffn2_sample")
        outs_s.append((heads_last(kt, db)[:, None], heads_last(vt, db)[:, None], lft.T[:, None],
                       st_new.reshape(db, nhs, p_dim, n_state), jnp.transpose(conv_new, (1, 0, 2))))

    stack = lambda outs, i: jnp.stack([o[i] for o in outs])
    return (xp.reshape(bsz, t, d), xs_.reshape(db, 1, d),
            stack(outs_p, 0), stack(outs_p, 1), stack(outs_p, 2), stack(outs_p, 3), stack(outs_p, 4),
            stack(outs_s, 0), stack(outs_s, 1), stack(outs_s, 2), stack(outs_s, 3), stack(outs_s, 4))
```

```python
import functools

import math

import jax
import jax.numpy as jnp
import numpy as np
from jax import lax
from jax.experimental import pallas as pl
from jax.experimental.pallas import tpu as pltpu

F32 = jnp.float32
BF16 = jnp.bfloat16
EPS = 1e-6
LOG2E = math.log2(math.e)
LANES = 128
SUBLANES = 8
SSD_CHUNK = 128
VMEM_LIMIT = 56 * 1024 * 1024


def _cparams(*semantics):
    return pltpu.CompilerParams(dimension_semantics=semantics, vmem_limit_bytes=VMEM_LIMIT)


def _dot(a, b):
    return jnp.dot(a.astype(BF16), b.astype(BF16), preferred_element_type=F32)


def _dot_nt(a, b):
    return lax.dot_general(a.astype(BF16), b.astype(BF16), (((1,), (1,)), ((), ())),
                           preferred_element_type=F32)


def _split3(a):
    hi = a.astype(BF16)
    r = a - hi.astype(F32)
    mid = r.astype(BF16)
    lo = (r - mid.astype(F32)).astype(BF16)
    return hi, mid, lo


def _xdot_l(a, e):
    hi, mid, lo = _split3(a)
    f = functools.partial(jnp.dot, preferred_element_type=F32)
    return f(hi, e) + f(mid, e) + f(lo, e)


def _xdot_r(e, b):
    hi, mid, lo = _split3(b)
    f = functools.partial(jnp.dot, preferred_element_type=F32)
    return f(e, hi) + f(e, mid) + f(e, lo)


def _silu(x):
    return x * jax.nn.sigmoid(x)


def _softplus(x):
    return jnp.maximum(x, 0.0) + jnp.log1p(jnp.exp(-jnp.abs(x)))


def _rms(x):
    return x * lax.rsqrt(jnp.mean(x * x, axis=-1, keepdims=True) + EPS)


def _norm_mod(x, nw, shift, scale):
    return (_rms(x) * nw) * (1.0 + scale) + shift


def _expand_matrix(n_rows, n_heads, width):
    row = lax.broadcasted_iota(jnp.int32, (n_rows, n_heads * width), 0)
    col = lax.broadcasted_iota(jnp.int32, (n_rows, n_heads * width), 1)
    return (col // width == row).astype(BF16)


def _tri_matrix(n):
    row = lax.broadcasted_iota(jnp.int32, (n, n), 0)
    col = lax.broadcasted_iota(jnp.int32, (n, n), 1)
    return (col <= row).astype(BF16)


def _gated_group_norm(y, xs, z, dskip, gnorm, n_groups):
    y = (y + dskip * xs) * _silu(z)
    gw = y.shape[1] // n_groups
    parts = [_rms(y[:, g * gw:(g + 1) * gw]) for g in range(n_groups)]
    return (jnp.concatenate(parts, axis=1) * gnorm).astype(BF16)


def _ada_kernel(c_ref, w_ref, b_ref, o_ref):
    o_ref[...] = _dot(_silu(c_ref[...]), w_ref[...]) + b_ref[...]


def _ada_call(c_all, w, b):
    r, d = c_all.shape
    n = w.shape[1]
    tn = d
    return pl.pallas_call(
        _ada_kernel,
        grid=(n // tn,),
        in_specs=[pl.BlockSpec((r, d), lambda j: (0, 0)),
                  pl.BlockSpec((d, tn), lambda j: (0, j)),
                  pl.BlockSpec((1, tn), lambda j: (0, j))],
        out_specs=pl.BlockSpec((r, tn), lambda j: (0, j)),
        out_shape=jax.ShapeDtypeStruct((r, n), F32),
        compiler_params=_cparams("parallel"),
        name="ada",
    )(c_all, w, b.reshape(1, n))


def _mod_spec(per_row, tm, d, idx, prompt_row_block):
    if per_row:
        return pl.BlockSpec((tm, d), lambda i: (i, idx))
    return pl.BlockSpec((SUBLANES, d), lambda i: (prompt_row_block, idx))


def _ffn_kernel(*refs, per_row, with_mix, final_norm, ff, tf, datt):
    it = iter(refs)
    x_ref = next(it)
    if with_mix:
        att_ref, ys_ref, wo_ref, g2_ref = next(it), next(it), next(it), next(it)
    sh_ref, sc_ref, g_ref, nw_ref, win_ref, wout_ref = (next(it) for _ in range(6))
    nf_ref = next(it) if final_norm else None
    o_ref = next(it)
    a_ref = next(it)
    rows = slice(None) if per_row else slice(0, 1)

    x = x_ref[...]
    if with_mix:
        mix = (jnp.dot(att_ref[...].astype(BF16), wo_ref[0:datt, :], preferred_element_type=F32)
               + jnp.dot(ys_ref[...], wo_ref[datt:, :], preferred_element_type=F32))
        x = x + g2_ref[rows, :] * mix
    h = _norm_mod(x, nw_ref[...], sh_ref[rows, :], sc_ref[rows, :]).astype(BF16)
    for c in range(ff // tf):
        g = jnp.dot(h, win_ref[:, c * tf:(c + 1) * tf], preferred_element_type=F32)
        u = jnp.dot(h, win_ref[:, ff + c * tf:ff + (c + 1) * tf], preferred_element_type=F32)
        a_ref[:, c * tf:(c + 1) * tf] = (_silu(g) * u).astype(BF16)
    y = x + (0.5 * g_ref[rows, :]) * jnp.dot(a_ref[...], wout_ref[...], preferred_element_type=F32)
    if final_norm:
        y = _rms(y) * nf_ref[...]
    o_ref[...] = y


def _ffn_call(x, m_all, ada_base, norm_w, w_in_b, w_out_b, *, per_row, tm, prompt_row_block,
              mix=None, final_norm_w=None, name):
    r, d = x.shape
    ff = w_out_b.shape[0]
    tf = 256
    assert r % tm == 0 and ff % tf == 0
    const = lambda i: (0, 0)
    resident = functools.partial(pl.BlockSpec, index_map=const, pipeline_mode=pl.Buffered(1))
    mspec = functools.partial(_mod_spec, per_row, tm, d, prompt_row_block=prompt_row_block)
    args, specs = [x], [pl.BlockSpec((tm, d), lambda i: (i, 0))]
    datt = 0
    if mix is not None:
        att, ys, wo_b, g2_idx = mix
        datt = att.shape[1]
        args += [att, ys, wo_b, m_all]
        specs += [pl.BlockSpec((tm, datt), lambda i: (i, 0)),
                  pl.BlockSpec((tm, ys.shape[1]), lambda i: (i, 0)),
                  resident(wo_b.shape), mspec(g2_idx)]
    args += [m_all, m_all, m_all, norm_w.reshape(1, d), w_in_b, w_out_b]
    specs += [mspec(ada_base), mspec(ada_base + 1), mspec(ada_base + 2),
              pl.BlockSpec((1, d), const), resident(w_in_b.shape), resident(w_out_b.shape)]
    if final_norm_w is not None:
        args.append(final_norm_w.reshape(1, d))
        specs.append(pl.BlockSpec((1, d), const))
    kern = functools.partial(_ffn_kernel, per_row=per_row, with_mix=mix is not None,
                             final_norm=final_norm_w is not None, ff=ff, tf=tf, datt=datt)
    return pl.pallas_call(
        kern,
        grid=(r // tm,),
        in_specs=specs,
        out_specs=pl.BlockSpec((tm, d), lambda i: (i, 0)),
        out_shape=jax.ShapeDtypeStruct((r, d), F32),
        scratch_shapes=[pltpu.VMEM((tm, ff), BF16)],
        compiler_params=_cparams("parallel"),
        name=name,
    )(*args)


def _project(x, sh, sc, nw_ref, wqkv_ref, wzx_ref, wfd_ref, bfd_ref, *, datt, dssd, nh, nhs):
    h = _norm_mod(x, nw_ref[...], sh, sc).astype(BF16)
    qkv = jnp.dot(h, wqkv_ref[...], preferred_element_type=F32)
    zx = jnp.dot(h, wzx_ref[...], preferred_element_type=F32)
    fd = jnp.dot(h, wfd_ref[...], preferred_element_type=F32) + bfd_ref[...]
    lane = lax.broadcasted_iota(jnp.int32, (1, LANES), 1)
    logf = jnp.where(lane < nh, -_softplus(-fd[:, :LANES]), 0.0)
    dt = jnp.where(lane < nhs, _softplus(fd[:, LANES:]), 0.0)
    return (qkv[:, :datt], qkv[:, datt:2 * datt], qkv[:, 2 * datt:], zx[:, :dssd], zx[:, dssd:],
            logf, dt)


def _inproj_prompt_kernel(x_ref, sh_ref, sc_ref, nw_ref, wqkv_ref, wzx_ref, wfd_ref, bfd_ref,
                          cw_ref, cb_ref, selq_ref, qone_ref, selk_ref, kone_ref,
                          qa_ref, ka_ref, vbt_ref, kt_ref, vt_ref, lft_ref, z_ref,
                          xc_ref, dt_ref, tail_ref,
                          ptail_ref, cf_ref, *, datt, dssd, nh, nhs, hd, scale, conv_w):
    @pl.when(pl.program_id(0) == 0)
    def _():
        ptail_ref[...] = jnp.zeros_like(ptail_ref)
        cf_ref[...] = jnp.zeros_like(cf_ref)

    tm = x_ref.shape[0]
    q, k, v, z, xbc, logf, dt = _project(
        x_ref[...], sh_ref[0:1, :], sc_ref[0:1, :], nw_ref, wqkv_ref, wzx_ref, wfd_ref, bfd_ref,
        datt=datt, dssd=dssd, nh=nh, nhs=nhs)
    kt_ref[...] = k.T
    vt = v.T
    vt_ref[...] = vt
    for h in range(nh):
        vbt_ref[h] = vt[h * hd:(h + 1) * hd, :].astype(BF16)
    z_ref[...] = z
    lft_ref[...] = logf.T[0:SUBLANES, :]
    dt_ref[...] = dt

    fc = _xdot_r(_tri_matrix(tm), logf) + cf_ref[...]
    cf_ref[...] = fc[tm - 1:tm, :]

    per = LANES // hd
    lane = lax.broadcasted_iota(jnp.int32, (1, LANES), 1)
    f3 = jnp.concatenate(_split3(fc * LOG2E), axis=1)
    qs = q * (scale * LOG2E)
    qbias = (jnp.dot(f3, selq_ref[...], preferred_element_type=F32) + qone_ref[...]).astype(BF16)
    kbias = (jnp.dot(f3, selk_ref[...], preferred_element_type=F32) + kone_ref[...]).astype(BF16)
    for h in range(nh):
        g, hh = divmod(h, per)
        qpair = qs[:, g * LANES:(g + 1) * LANES]
        qa_ref[h, :, 0:LANES] = jnp.where(lane // hd == hh, qpair, 0.0).astype(BF16)
        qa_ref[h, :, LANES:2 * LANES] = qbias[:, h * LANES:(h + 1) * LANES]
    for g in range(nh // per):
        ka_ref[g, :, 0:LANES] = k[:, g * LANES:(g + 1) * LANES].astype(BF16)
        ka_ref[g, :, LANES:2 * LANES] = kbias[:, g * LANES:(g + 1) * LANES]

    ptail = ptail_ref[...]
    row8 = lax.broadcasted_iota(jnp.int32, ptail.shape, 0)
    acc = cb_ref[...]
    for i in range(conv_w):
        s = conv_w - 1 - i
        if s == 0:
            xs = xbc
        else:
            r = pltpu.roll(xbc, s, 0)
            top = jnp.where(row8 < s, pltpu.roll(ptail, s, 0), r[0:SUBLANES])
            xs = jnp.concatenate([top, r[SUBLANES:]], axis=0)
        acc = acc + xs * cw_ref[i:i + 1, :]
    xc_ref[...] = _silu(acc)
    tail = xbc[tm - SUBLANES:tm]
    ptail_ref[...] = tail
    tail_ref[...] = tail


def _bias_selectors(nh, hd):
    per = LANES // hd
    selq = np.zeros((3 * LANES, nh, LANES), np.float32)
    qone = np.zeros((nh, LANES), np.float32)
    selk = np.zeros((3 * LANES, nh // per, LANES), np.float32)
    kone = np.zeros((nh // per, LANES), np.float32)
    kone[:, 0:3] = 1.0
    for h in range(nh):
        g, hh = divmod(h, per)
        for c in range(3):
            selq[c * LANES + h, h, c] = 1.0
            selk[c * LANES + h, g, 3 + 3 * hh + c] = -1.0
        qone[h, 3 + 3 * hh:6 + 3 * hh] = 1.0
    return (jnp.asarray(selq.reshape(3 * LANES, -1), BF16), jnp.asarray(qone.reshape(1, -1), F32),
            jnp.asarray(selk.reshape(3 * LANES, -1), BF16), jnp.asarray(kone.reshape(1, -1), F32))


def _inproj_prompt_call(x, m_all, prompt_row_block, norm_w, wqkv, wzx, wfd, bfd, cw, cb, *,
                        datt, dssd, nh, nhs, hd, scale, tm):
    t, d = x.shape
    dconv = cw.shape[1]
    per = LANES // hd
    assert t % tm == 0 and nh == SUBLANES and conv_w_ok(cw.shape[0]) and 3 + 3 * per <= LANES
    const = lambda i: (0, 0)
    full = lambda a: pl.BlockSpec(a.shape, lambda i: (0,) * a.ndim)
    rowblk = lambda w: pl.BlockSpec((tm, w), lambda i: (i, 0))
    colblk = lambda h: pl.BlockSpec((h, tm), lambda i: (0, i))
    mspec = functools.partial(_mod_spec, False, tm, d, prompt_row_block=prompt_row_block)
    kern = functools.partial(_inproj_prompt_kernel, datt=datt, dssd=dssd, nh=nh, nhs=nhs, hd=hd,
                             scale=scale, conv_w=cw.shape[0])
    sel = _bias_selectors(nh, hd)
    sds = jax.ShapeDtypeStruct
    return pl.pallas_call(
        kern,
        grid=(t // tm,),
        in_specs=[rowblk(d), mspec(3), mspec(4), pl.BlockSpec((1, d), const),
                  full(wqkv), full(wzx), full(wfd), full(bfd), full(cw),
                  pl.BlockSpec((1, dconv), const)] + [full(a) for a in sel],
        out_specs=[pl.BlockSpec((nh, tm, 2 * LANES), lambda i: (0, i, 0)),
                   pl.BlockSpec((nh // per, tm, 2 * LANES), lambda i: (0, i, 0)),
                   pl.BlockSpec((nh, hd, tm), lambda i: (0, 0, i)),
                   colblk(datt), colblk(datt), colblk(SUBLANES),
                   rowblk(dssd), rowblk(dconv), rowblk(LANES),
                   pl.BlockSpec((SUBLANES, dconv), const)],
        out_shape=[sds((nh, t, 2 * LANES), BF16), sds((nh // per, t, 2 * LANES), BF16),
                   sds((nh, hd, t), BF16), sds((datt, t), F32), sds((datt, t), F32),
                   sds((SUBLANES, t), F32), sds((t, dssd), F32),
                   sds((t, dconv), F32), sds((t, LANES), F32), sds((SUBLANES, dconv), F32)],
        scratch_shapes=[pltpu.VMEM((SUBLANES, dconv), F32), pltpu.VMEM((1, LANES), F32)],
        compiler_params=_cparams("arbitrary"),
        name="inproj_prompt",
    )(x, m_all, m_all, norm_w.reshape(1, d), wqkv, wzx, wfd, bfd, cw, cb.reshape(1, dconv), *sel)


def conv_w_ok(w):
    return 1 <= w - 1 < SUBLANES


def _inproj_sample_kernel(x_ref, sh_ref, sc_ref, nw_ref, wqkv_ref, wzx_ref, wfd_ref, bfd_ref,
                          cw_ref, cb_ref, cs_ref,
                          q_ref, k_ref, v_ref, kt_ref, vt_ref, lf_ref, lft_ref, z_ref, xc_ref, dt_ref,
                          cn_ref, *, datt, dssd, nh, nhs, scale, conv_w):
    q, k, v, z, xbc, logf, dt = _project(
        x_ref[...], sh_ref[...], sc_ref[...], nw_ref, wqkv_ref, wzx_ref, wfd_ref, bfd_ref,
        datt=datt, dssd=dssd, nh=nh, nhs=nhs)
    q_ref[...] = q * scale
    k_ref[...] = k
    v_ref[...] = v
    kt_ref[...] = k.T
    vt_ref[...] = v.T
    z_ref[...] = z
    lf_ref[...] = logf[:, 0:nh]
    lft_ref[...] = logf.T[0:SUBLANES, :]
    dt_ref[...] = dt
    acc = cb_ref[...]
    for i in range(conv_w - 1):
        acc = acc + cs_ref[i] * cw_ref[i:i + 1, :]
    acc = acc + xbc * cw_ref[conv_w - 1:conv_w, :]
    xc_ref[...] = _silu(acc)
    for i in range(conv_w - 2):
        cn_ref[i] = cs_ref[i + 1]
    cn_ref[conv_w - 2] = xbc


def _inproj_sample_call(x, m_all, norm_w, wqkv, wzx, wfd, bfd, cw, cb, conv_state, *,
                        datt, dssd, nh, nhs, scale):
    r, d = x.shape
    conv_w, dconv = cw.shape
    const = lambda i: (0, 0)
    full = lambda a: pl.BlockSpec(a.shape, lambda i: (0,) * a.ndim)
    mspec = functools.partial(_mod_spec, True, r, d, prompt_row_block=0)
    kern = functools.partial(_inproj_sample_kernel, datt=datt, dssd=dssd, nh=nh, nhs=nhs,
                             scale=scale, conv_w=conv_w)
    sds = jax.ShapeDtypeStruct
    shapes = [sds((r, datt), F32), sds((r, datt), F32), sds((r, datt), F32), sds((datt, r), F32),
              sds((datt, r), F32), sds((r, nh), F32), sds((SUBLANES, r), F32), sds((r, dssd), F32),
              sds((r, dconv), F32), sds((r, LANES), F32), sds(conv_state.shape, F32)]
    return pl.pallas_call(
        kern,
        grid=(1,),
        in_specs=[full(x), mspec(3), mspec(4), pl.BlockSpec((1, d), const), full(wqkv), full(wzx),
                  full(wfd), full(bfd), full(cw), pl.BlockSpec((1, dconv), const), full(conv_state)],
        out_specs=[full(s) for s in shapes],
        out_shape=shapes,
        compiler_params=_cparams("arbitrary"),
        name="inproj_sample",
    )(x, m_all, m_all, norm_w.reshape(1, d), wqkv, wzx, wfd, bfd, cw, cb.reshape(1, dconv), conv_state)


def _attn_prompt_kernel(qi_ref, kj_ref, q_ref, k_ref, vt_ref, o_ref, m_ref, l_ref, acc_ref, *, nh, hd):
    step_id = pl.program_id(0)
    i = qi_ref[step_id]
    j = kj_ref[step_id]
    tq, tk = q_ref.shape[1], k_ref.shape[1]
    kpq = tq // tk
    per = LANES // hd

    @pl.when(j == 0)
    def _():
        m_ref[...] = jnp.full_like(m_ref, -jnp.inf)
        l_ref[...] = jnp.zeros_like(l_ref)
        acc_ref[...] = jnp.zeros_like(acc_ref)

    ones = jnp.ones((2 * SUBLANES, tk), BF16)

    def step(masked):
        if masked:
            keep = (lax.broadcasted_iota(jnp.int32, (tk, tq), 0)
                    - lax.broadcasted_iota(jnp.int32, (tk, tq), 1)) <= i * tq - j * tk

        def scores(h):
            return lax.dot_general(k_ref[h // per], q_ref[h], (((1,), (1,)), ((), ())),
                                   preferred_element_type=F32)

        def softmax(h, st):
            if masked:
                st = jnp.where(keep, st, -jnp.inf)
            m_old = m_ref[h]
            m_new = jnp.maximum(m_old, jnp.max(st, axis=0, keepdims=True))
            m_ref[h] = m_new
            return jnp.exp2(st - m_new).astype(BF16), jnp.exp2(m_old - m_new)

        def values(h, p, alpha):
            pv = jnp.dot(jnp.concatenate([vt_ref[h], ones], axis=0), p,
                         preferred_element_type=F32)
            acc_ref[h] = alpha * acc_ref[h] + pv[0:hd]
            l_ref[h] = alpha * l_ref[h] + pv[hd:hd + 1]

        st = {0: scores(0)}
        pa = {}
        for h in range(nh + 1):
            if h + 1 < nh:
                st[h + 1] = scores(h + 1)
            if h < nh:
                pa[h] = softmax(h, st.pop(h))
            if h >= 1:
                values(h - 1, *pa.pop(h - 1))

    @pl.when(j < kpq * i)
    def _():
        step(False)

    @pl.when(j >= kpq * i)
    def _():
        step(True)

    @pl.when(j == kpq * (i + 1) - 1)
    def _():
        out_t = jnp.concatenate([acc_ref[h] * (1.0 / l_ref[h]) for h in range(nh)], axis=0)
        o_ref[...] = out_t.T.astype(BF16)


def _attn_prompt_call(qa, ka, vbt, *, nh, hd, tq, tk):
    t = qa.shape[1]
    datt = nh * hd
    per = LANES // hd
    assert t % tq == 0 and tq % tk == 0 and LANES % hd == 0 and nh % per == 0
    nq, kpq = t // tq, tq // tk
    qi = np.concatenate([np.full(kpq * (i + 1), i, np.int32) for i in range(nq)])
    kj = np.concatenate([np.arange(kpq * (i + 1), dtype=np.int32) for i in range(nq)])
    kern = functools.partial(_attn_prompt_kernel, nh=nh, hd=hd)
    return pl.pallas_call(
        kern,
        grid_spec=pltpu.PrefetchScalarGridSpec(
            num_scalar_prefetch=2,
            grid=(len(qi),),
            in_specs=[pl.BlockSpec((nh, tq, qa.shape[2]), lambda s, qi, kj: (0, qi[s], 0)),
                      pl.BlockSpec((nh // per, tk, ka.shape[2]), lambda s, qi, kj: (0, kj[s], 0)),
                      pl.BlockSpec((nh, hd, tk), lambda s, qi, kj: (0, 0, kj[s]))],
            out_specs=pl.BlockSpec((tq, datt), lambda s, qi, kj: (qi[s], 0)),
            scratch_shapes=[pltpu.VMEM((nh, 1, tq), F32), pltpu.VMEM((nh, 1, tq), F32),
                            pltpu.VMEM((nh, hd, tq), F32)]),
        out_shape=jax.ShapeDtypeStruct((t, datt), BF16),
        compiler_params=_cparams("arbitrary"),
        name="attn_prompt",
    )(jnp.asarray(qi), jnp.asarray(kj), qa, ka, vbt)


def _decode_kernel(pt_ref, q_ref, kn_ref, vn_ref, lfn_ref, ck_hbm, cv_hbm, clf_hbm, o_ref,
                   kbuf, vbuf, lfbuf, sem, qcol_ref, acc_ref, *, nh, hd, pg, layer_base):
    b = pl.program_id(0)
    n_seq = pl.num_programs(0)
    _, _, datt, page = kbuf.shape
    n_groups = pt_ref.shape[1] // pg
    streams = ((ck_hbm, kbuf), (cv_hbm, vbuf), (clf_hbm, lfbuf))

    def page_copy(k, seq, g, slot, p):
        src, dst = streams[k]
        return pltpu.make_async_copy(src.at[layer_base + pt_ref[seq, g * pg + p]], dst.at[slot, p],
                                     sem.at[slot, k])

    def start_group(seq, g, slot):
        for p in range(pg):
            for k in range(len(streams)):
                page_copy(k, seq, g, slot, p).start()

    def wait_group(seq, g, slot):
        for p in range(pg):
            for k in range(len(streams)):
                page_copy(k, seq, g, slot, p).wait()

    def head_rows(a):
        return jnp.concatenate([jnp.broadcast_to(a[h:h + 1, :], (hd, a.shape[1])) for h in range(nh)],
                               axis=0)

    @pl.when(b == 0)
    def _():
        start_group(0, 0, 0)

    qcol_ref[...] = jnp.broadcast_to(q_ref[...], (page, datt)).T
    acc_ref[...] = jnp.zeros_like(acc_ref)
    triu = (lax.broadcasted_iota(jnp.int32, (page, page), 0)
            <= lax.broadcasted_iota(jnp.int32, (page, page), 1)).astype(BF16)
    m_run = jnp.full((nh, 1), -jnp.inf, F32)
    l_run = jnp.zeros((nh, 1), F32)
    carry = jnp.zeros((nh, page), F32)

    for g in range(n_groups):
        slot = g % 2
        if g + 1 < n_groups:
            start_group(b, g + 1, 1 - slot)
        else:
            @pl.when(b + 1 < n_seq)
            def _():
                start_group(b + 1, 0, 1 - slot)
        wait_group(b, g, slot)

        f_loc = _xdot_l(jnp.concatenate([lfbuf[slot, p] for p in range(pg)], axis=0), triu)
        f_tot = jnp.broadcast_to(f_loc[:, page - 1:page], f_loc.shape)
        scores = []
        for p in range(pg):
            s = jnp.concatenate(
                [jnp.sum(kbuf[slot, p, h * hd:(h + 1) * hd, :] * qcol_ref[h * hd:(h + 1) * hd, :],
                         axis=0, keepdims=True) for h in range(nh)], axis=0)
            scores.append(s - (f_loc[p * nh:(p + 1) * nh, :] + carry))
            carry = carry + f_tot[p * nh:(p + 1) * nh, :]
        m_old = m_run
        for s in scores:
            m_run = jnp.maximum(m_run, jnp.max(s, axis=1, keepdims=True))
        alpha = jnp.exp(m_old - m_run)
        l_run = alpha * l_run
        probs = []
        for s in scores:
            pr = jnp.exp(s - m_run)
            l_run = l_run + jnp.sum(pr, axis=1, keepdims=True)
            probs.append(pr)
        for h in range(nh):
            hr = slice(h * hd, (h + 1) * hd)
            a = acc_ref[hr, :] * alpha[h:h + 1, :]
            for p in range(pg):
                a = a + vbuf[slot, p, hr, :] * probs[p][h:h + 1, :]
            acc_ref[hr, :] = a

    row = lax.broadcasted_iota(jnp.int32, (nh, datt), 0)
    col = lax.broadcasted_iota(jnp.int32, (nh, datt), 1)
    qk = jnp.where(col // hd == row, jnp.broadcast_to(q_ref[...] * kn_ref[...], (nh, datt)), 0.0)
    s_new = jnp.sum(qk, axis=1, keepdims=True) - (carry[:, 0:1] + lfn_ref[...])
    m_fin = jnp.maximum(m_run, s_new)
    a_fin = jnp.exp(m_run - m_fin)
    p_new = jnp.exp(s_new - m_fin)
    l_fin = a_fin * l_run + p_new
    tot = jnp.sum(acc_ref[...], axis=1, keepdims=True)
    vcol = jnp.broadcast_to(vn_ref[...], (LANES, datt)).T[:, 0:1]
    out = (tot * head_rows(a_fin) + head_rows(p_new.astype(BF16).astype(F32)) * vcol) \
        / head_rows(l_fin)
    o_ref[...] = jnp.broadcast_to(out, (datt, LANES)).T[0:1, :]


def _decode_call(page_table, q, kn, vn, lfn, ckt, cvt, clft, layer_base, *, nh, hd, pg):
    db, datt = q.shape
    n_pages = page_table.shape[1]
    page = ckt.shape[2]
    assert n_pages % pg == 0 and (n_pages // pg) % 2 == 0 and page == LANES
    row3 = lambda a: a.reshape(db, 1, a.shape[1])
    rowspec = lambda w: pl.BlockSpec((None, 1, w), lambda b, pt: (b, 0, 0))
    hbm = pl.BlockSpec(memory_space=pl.ANY)
    kern = functools.partial(_decode_kernel, nh=nh, hd=hd, pg=pg, layer_base=layer_base)
    out = pl.pallas_call(
        kern,
        grid_spec=pltpu.PrefetchScalarGridSpec(
            num_scalar_prefetch=1,
            grid=(db,),
            in_specs=[rowspec(datt), rowspec(datt), rowspec(datt),
                      pl.BlockSpec((None, nh, 1), lambda b, pt: (b, 0, 0)), hbm, hbm, hbm],
            out_specs=rowspec(datt),
            scratch_shapes=[pltpu.VMEM((2, pg, datt, page), F32), pltpu.VMEM((2, pg, datt, page), F32),
                            pltpu.VMEM((2, pg, nh, page), F32), pltpu.SemaphoreType.DMA((2, 3)),
                            pltpu.VMEM((datt, page), F32), pltpu.VMEM((datt, page), F32)]),
        out_shape=jax.ShapeDtypeStruct((db, 1, datt), F32),
        compiler_params=_cparams("arbitrary"),
        name="attn_sample",
    )(page_table, row3(q), row3(kn), row3(vn), lfn.reshape(db, nh, 1), ckt, cvt, clft)
    return out.reshape(db, datt)


def _ssd_prompt_kernel(xc_ref, dt_ref, z_ref, alog_ref, dskip_ref, gn_ref, y_ref, hl_ref, st_ref,
                       *, nhs, p_dim, n_state, n_groups):
    @pl.when(pl.program_id(0) == 0)
    def _():
        st_ref[...] = jnp.zeros_like(st_ref)

    for c in range(xc_ref.shape[0] // SSD_CHUNK):
        _ssd_chunk(xc_ref, dt_ref, z_ref, alog_ref, dskip_ref, gn_ref, y_ref, st_ref,
                   slice(c * SSD_CHUNK, (c + 1) * SSD_CHUNK), nhs=nhs, p_dim=p_dim,
                   n_state=n_state, n_groups=n_groups)
    hl_ref[...] = st_ref[...]


def _ssd_chunk(xc_ref, dt_ref, z_ref, alog_ref, dskip_ref, gn_ref, y_ref, st_ref, rs,
               *, nhs, p_dim, n_state, n_groups):
    ln = rs.stop - rs.start
    dssd = nhs * p_dim
    hpg = nhs // n_groups
    per = LANES // p_dim
    xs = xc_ref[rs, 0:dssd]
    dt = dt_ref[rs, :]
    a = -jnp.exp(alog_ref[...])
    expand = _expand_matrix(LANES, nhs, p_dim)
    acum = _xdot_r(_tri_matrix(ln), dt * a)
    acum_t = acum.T
    acum_x = _xdot_l(acum, expand)
    xdt = xs * _xdot_l(dt, expand)
    a_last_x = acum_x[ln - 1:ln, :]
    xdt_end = xdt * jnp.exp(a_last_x - acum_x)
    cdec = jnp.exp(acum_t[:, ln - 1:ln])
    causal = (lax.broadcasted_iota(jnp.int32, (ln, ln), 1)
              <= lax.broadcasted_iota(jnp.int32, (ln, ln), 0))
    lane = lax.broadcasted_iota(jnp.int32, (1, LANES), 1)

    y_parts = []
    for g in range(n_groups):
        bg = xc_ref[rs, dssd + g * n_state:dssd + (g + 1) * n_state]
        cg = xc_ref[rs, dssd + (n_groups + g) * n_state:dssd + (n_groups + g + 1) * n_state]
        gw = hpg * p_dim
        rows = slice(g * gw, (g + 1) * gw)
        st_g = st_ref[rows, :]
        cb = _dot_nt(cg, bg)
        y_off = _dot_nt(cg, st_g) * jnp.exp(acum_x[:, rows])
        for q in range(hpg // per):
            lsl = slice(g * gw + q * LANES, g * gw + (q + 1) * LANES)
            xdt_q = xdt[:, lsl].astype(BF16)
            y_q = jnp.zeros((ln, LANES), F32)
            for hh in range(per):
                h = g * hpg + q * per + hh
                seg = acum[:, h:h + 1] - acum_t[h:h + 1, :]
                m = cb * jnp.exp(jnp.where(causal, seg, -jnp.inf))
                yd = jnp.dot(m.astype(BF16), xdt_q, preferred_element_type=F32)
                y_q = jnp.where(lane // p_dim == hh, yd, y_q)
            y_parts.append(y_q + y_off[:, q * LANES:(q + 1) * LANES])
        cs = jnp.dot(xdt_end[:, rows].T.astype(BF16), bg.astype(BF16), preferred_element_type=F32)
        for hh in range(hpg):
            h = g * hpg + hh
            hr = slice(h * p_dim, (h + 1) * p_dim)
            dec = jnp.broadcast_to(cdec[h:h + 1, :], (p_dim, n_state))
            st_ref[hr, :] = st_ref[hr, :] * dec + cs[hh * p_dim:(hh + 1) * p_dim, :]
    y = jnp.concatenate(y_parts, axis=1)
    y_ref[rs, :] = _gated_group_norm(y, xs, z_ref[rs, :], dskip_ref[...], gn_ref[...], n_groups)


def _ssd_prompt_call(xc, dt, z, alog_pad, dskip_x, gnorm, *, nhs, p_dim, n_state, n_groups):
    t, dconv = xc.shape
    dssd = nhs * p_dim
    ln = 2 * SSD_CHUNK if t % (2 * SSD_CHUNK) == 0 else SSD_CHUNK
    assert t % ln == 0 and LANES % p_dim == 0 and (nhs // n_groups) % (LANES // p_dim) == 0
    const = lambda i: (0, 0)
    kern = functools.partial(_ssd_prompt_kernel, nhs=nhs, p_dim=p_dim, n_state=n_state,
                             n_groups=n_groups)
    return pl.pallas_call(
        kern,
        grid=(t // ln,),
        in_specs=[pl.BlockSpec((ln, dconv), lambda i: (i, 0)),
                  pl.BlockSpec((ln, LANES), lambda i: (i, 0)),
                  pl.BlockSpec((ln, dssd), lambda i: (i, 0)),
                  pl.BlockSpec((1, LANES), const), pl.BlockSpec((1, dssd), const),
                  pl.BlockSpec((1, dssd), const)],
        out_specs=[pl.BlockSpec((ln, dssd), lambda i: (i, 0)),
                   pl.BlockSpec((dssd, n_state), const)],
        out_shape=[jax.ShapeDtypeStruct((t, dssd), BF16), jax.ShapeDtypeStruct((dssd, n_state), F32)],
        scratch_shapes=[pltpu.VMEM((dssd, n_state), F32)],
        compiler_params=_cparams("arbitrary"),
        name="ssd_prompt",
    )(xc, dt, z, alog_pad, dskip_x, gnorm)


def _ssd_sample_kernel(xc_ref, dt_ref, z_ref, alog_ref, dskip_ref, gn_ref, st_ref, y_ref, so_ref,
                       *, nhs, p_dim, n_state, n_groups):
    bb = xc_ref.shape[0]
    dssd = nhs * p_dim
    gw = dssd // n_groups
    xs = xc_ref[:, 0:dssd]
    dt = dt_ref[...]
    a = -jnp.exp(alog_ref[...])
    expand = _expand_matrix(LANES, nhs, p_dim)
    dec_x = _xdot_l(jnp.exp(dt * a), expand)
    xdt = xs * _xdot_l(dt, expand)

    def column(row):
        return jnp.broadcast_to(row, (n_state, dssd)).T

    ys = []
    for r in range(bb):
        bfull = jnp.concatenate(
            [jnp.broadcast_to(xc_ref[r:r + 1, dssd + g * n_state:dssd + (g + 1) * n_state],
                              (gw, n_state)) for g in range(n_groups)], axis=0)
        cfull = jnp.concatenate(
            [jnp.broadcast_to(xc_ref[r:r + 1, dssd + (n_groups + g) * n_state:
                                     dssd + (n_groups + g + 1) * n_state],
                              (gw, n_state)) for g in range(n_groups)], axis=0)
        s_new = st_ref[r] * column(dec_x[r:r + 1, :]) + column(xdt[r:r + 1, :]) * bfull
        so_ref[r] = s_new
        ycol = jnp.sum(s_new * cfull, axis=1, keepdims=True)
        ys.append(jnp.broadcast_to(ycol, (dssd, n_state)).T[0:1, :])
    y = jnp.concatenate(ys, axis=0)
    y_ref[...] = _gated_group_norm(y, xs, z_ref[...], dskip_ref[...], gn_ref[...], n_groups)


def _ssd_sample_call(xc, dt, z, alog_pad, dskip_x, gnorm, state, *, nhs, p_dim, n_state, n_groups):
    db, dconv = xc.shape
    dssd = nhs * p_dim
    bb = SUBLANES
    assert db % bb == 0 and n_state == LANES
    const = lambda i: (0, 0)
    kern = functools.partial(_ssd_sample_kernel, nhs=nhs, p_dim=p_dim, n_state=n_state,
                             n_groups=n_groups)
    return pl.pallas_call(
        kern,
        grid=(db // bb,),
        in_specs=[pl.BlockSpec((bb, dconv), lambda i: (i, 0)),
                  pl.BlockSpec((bb, LANES), lambda i: (i, 0)),
                  pl.BlockSpec((bb, dssd), lambda i: (i, 0)),
                  pl.BlockSpec((1, LANES), const), pl.BlockSpec((1, dssd), const),
                  pl.BlockSpec((1, dssd), const),
                  pl.BlockSpec((bb, dssd, n_state), lambda i: (i, 0, 0))],
        out_specs=[pl.BlockSpec((bb, dssd), lambda i: (i, 0)),
                   pl.BlockSpec((bb, dssd, n_state), lambda i: (i, 0, 0))],
        out_shape=[jax.ShapeDtypeStruct((db, dssd), BF16),
                   jax.ShapeDtypeStruct((db, dssd, n_state), F32)],
        compiler_params=_cparams("parallel"),
        name="ssd_sample",
    )(xc, dt, z, alog_pad, dskip_x, gnorm, state)


def _pad_lanes(v):
    return jnp.zeros((1, LANES), F32).at[0, :v.shape[0]].set(v)


def kernel(x_prompt, x_sample, cache_k, cache_v, cache_logf, state_ssm, state_conv, page_table, c_prompt, c_sample, w_ada, b_ada, norm_ffn1, w_ffn1_in, w_ffn1_out, norm_mix, w_in, b_forget, conv_w, conv_b, dt_bias, a_log, d_skip, ssd_norm, w_out, norm_ffn2, w_ffn2_in, w_ffn2_out, norm_final):
    bsz, t, d = x_prompt.shape
    db, s_new, _ = x_sample.shape
    depth, n_phys, page, nh, hd = cache_k.shape
    _, _, nhs, p_dim, n_state = state_ssm.shape
    conv_w1, dconv = state_conv.shape[2:]
    datt, dssd = nh * hd, nhs * p_dim
    n_groups = (dconv - dssd) // (2 * n_state)
    assert bsz == 1 and s_new == 1 and db % SUBLANES == 0 and c_prompt.shape[0] == 1
    scale = hd ** -0.5
    prompt_row_block = db // SUBLANES
    tm = min(512, t)
    n_pages = page_table.shape[1]
    pg = min(16, n_pages)

    xp = x_prompt.reshape(t, d)
    xs_ = x_sample.reshape(db, d)
    ckt = jnp.transpose(cache_k, (0, 1, 3, 4, 2)).reshape(depth * n_phys, datt, page)
    cvt = jnp.transpose(cache_v, (0, 1, 3, 4, 2)).reshape(depth * n_phys, datt, page)
    clft = jnp.transpose(cache_logf, (0, 1, 3, 2)).reshape(depth * n_phys, nh, page)
    heads_last = lambda at, n: jnp.transpose(at.reshape(nh, hd, n), (2, 0, 1))
    c_all = jnp.concatenate([c_sample, c_prompt, jnp.zeros((SUBLANES - 1, d), F32)], axis=0)

    outs_p, outs_s = [], []
    for l in range(depth):
        sp = (datt, 2 * datt, 3 * datt, 3 * datt + nh, 3 * datt + nh + dssd,
              3 * datt + nh + dssd + dconv)
        wi = w_in[l]
        wqkv = wi[:, :sp[2]].astype(BF16)
        wzx = wi[:, sp[3]:sp[5]].astype(BF16)
        wfd = (jnp.zeros((d, 2 * LANES), F32).at[:, :nh].set(wi[:, sp[2]:sp[3]])
               .at[:, LANES:LANES + nhs].set(wi[:, sp[5]:])).astype(BF16)
        bfd = jnp.concatenate([_pad_lanes(b_forget[l]), _pad_lanes(dt_bias[l])], axis=1)
        alog_pad = _pad_lanes(a_log[l])
        dskip_x = jnp.repeat(d_skip[l], p_dim).reshape(1, dssd)
        gnorm = ssd_norm[l].reshape(1, dssd)
        w1i, w1o = w_ffn1_in[l].astype(BF16), w_ffn1_out[l].astype(BF16)
        w2i, w2o = w_ffn2_in[l].astype(BF16), w_ffn2_out[l].astype(BF16)
        wo = w_out[l].astype(BF16)
        last = l == depth - 1

        m_all = _ada_call(c_all, w_ada[l], b_ada[l])
        proj = dict(datt=datt, dssd=dssd, nh=nh, nhs=nhs, scale=scale)
        ssd = dict(nhs=nhs, p_dim=p_dim, n_state=n_state, n_groups=n_groups)

        xp = _ffn_call(xp, m_all, 0, norm_ffn1[l], w1i, w1o, per_row=False, tm=tm,
                       prompt_row_block=prompt_row_block, name="ffn1_prompt")
        (qa, ka, vbt, kt, vt, lft, z, xc, dt, tail) = _inproj_prompt_call(
            xp, m_all, prompt_row_block, norm_mix[l], wqkv, wzx, wfd, bfd, conv_w[l], conv_b[l],
            tm=tm, hd=hd, **proj)
        att = _attn_prompt_call(qa, ka, vbt, nh=nh, hd=hd, tq=min(2 * tm, t), tk=tm)
        yssd, h_last = _ssd_prompt_call(xc, dt, z, alog_pad, dskip_x, gnorm, **ssd)
        xp = _ffn_call(xp, m_all, 6, norm_ffn2[l], w2i, w2o, per_row=False, tm=tm,
                       prompt_row_block=prompt_row_block, mix=(att, yssd, wo, 5),
                       final_norm_w=norm_final if last else None, name="ffn2_prompt")
        outs_p.append((heads_last(kt, t)[None], heads_last(vt, t)[None], lft.T[None],
                       h_last.reshape(1, nhs, p_dim, n_state),
                       tail[SUBLANES - conv_w1:].reshape(1, conv_w1, dconv)))

        xs_ = _ffn_call(xs_, m_all, 0, norm_ffn1[l], w1i, w1o, per_row=True, tm=db,
                        prompt_row_block=prompt_row_block, name="ffn1_sample")
        (q, k, v, kt, vt, logf, lft, z, xc, dt, conv_new) = _inproj_sample_call(
            xs_, m_all, norm_mix[l], wqkv, wzx, wfd, bfd, conv_w[l], conv_b[l],
            jnp.transpose(state_conv[l], (1, 0, 2)), **proj)
        att = _decode_call(page_table, q, k, v, logf, ckt, cvt, clft, l * n_phys,
                           nh=nh, hd=hd, pg=pg)
        yssd, st_new = _ssd_sample_call(xc, dt, z, alog_pad, dskip_x, gnorm,
                                        state_ssm[l].reshape(db, dssd, n_state), **ssd)
        xs_ = _ffn_call(xs_, m_all, 6, norm_ffn2[l], w2i, w2o, per_row=True, tm=db,
                        prompt_row_block=prompt_row_block, mix=(att, yssd, wo, 5),
                        final_norm_w=norm_final if last else None, name="ffn2_sample")
        outs_s.append((heads_last(kt, db)[:, None], heads_last(vt, db)[:, None], lft.T[:, None],
                       st_new.reshape(db, nhs, p_dim, n_state), jnp.transpose(conv_new, (1, 0, 2))))

    stack = lambda outs, i: jnp.stack([o[i] for o in outs])
    return (xp.reshape(bsz, t, d), xs_.reshape(db, 1, d),
            stack(outs_p, 0), stack(outs_p, 1), stack(outs_p, 2), stack(outs_p, 3), stack(outs_p, 4),
            stack(outs_s, 0), stack(outs_s, 1), stack(outs_s, 2), stack(outs_s, 3), stack(outs_s, 4))
```

```python
import functools

import math

import jax
import jax.numpy as jnp
import numpy as np
from jax import lax
from jax.experimental import pallas as pl
from jax.experimental.pallas import tpu as pltpu

F32 = jnp.float32
BF16 = jnp.bfloat16
EPS = 1e-6
LOG2E = math.log2(math.e)
LANES = 128
SUBLANES = 8
SSD_CHUNK = 128
VMEM_LIMIT = 56 * 1024 * 1024


def _cparams(*semantics):
    return pltpu.CompilerParams(dimension_semantics=semantics, vmem_limit_bytes=VMEM_LIMIT)


def _dot(a, b):
    return jnp.dot(a.astype(BF16), b.astype(BF16), preferred_element_type=F32)


def _dot_nt(a, b):
    return lax.dot_general(a.astype(BF16), b.astype(BF16), (((1,), (1,)), ((), ())),
                           preferred_element_type=F32)


def _split3(a):
    hi = a.astype(BF16)
    r = a - hi.astype(F32)
    mid = r.astype(BF16)
    lo = (r - mid.astype(F32)).astype(BF16)
    return hi, mid, lo


def _xdot_l(a, e):
    hi, mid, lo = _split3(a)
    f = functools.partial(jnp.dot, preferred_element_type=F32)
    return f(hi, e) + f(mid, e) + f(lo, e)


def _xdot_r(e, b):
    hi, mid, lo = _split3(b)
    f = functools.partial(jnp.dot, preferred_element_type=F32)
    return f(e, hi) + f(e, mid) + f(e, lo)


def _silu(x):
    return x * jax.nn.sigmoid(x)


def _softplus(x):
    return jnp.maximum(x, 0.0) + jnp.log1p(jnp.exp(-jnp.abs(x)))


def _rms(x):
    return x * lax.rsqrt(jnp.mean(x * x, axis=-1, keepdims=True) + EPS)


def _norm_mod(x, nw, shift, scale):
    return (_rms(x) * nw) * (1.0 + scale) + shift


def _expand_matrix(n_rows, n_heads, width):
    row = lax.broadcasted_iota(jnp.int32, (n_rows, n_heads * width), 0)
    col = lax.broadcasted_iota(jnp.int32, (n_rows, n_heads * width), 1)
    return (col // width == row).astype(BF16)


def _tri_matrix(n):
    row = lax.broadcasted_iota(jnp.int32, (n, n), 0)
    col = lax.broadcasted_iota(jnp.int32, (n, n), 1)
    return (col <= row).astype(BF16)


def _gated_group_norm(y, xs, z, dskip, gnorm, n_groups):
    y = (y + dskip * xs) * _silu(z)
    gw = y.shape[1] // n_groups
    parts = [_rms(y[:, g * gw:(g + 1) * gw]) for g in range(n_groups)]
    return (jnp.concatenate(parts, axis=1) * gnorm).astype(BF16)


def _ada_kernel(c_ref, w_ref, b_ref, o_ref):
    o_ref[...] = _dot(_silu(c_ref[...]), w_ref[...]) + b_ref[...]


def _ada_call(c_all, w, b):
    r, d = c_all.shape
    n = w.shape[1]
    tn = d
    return pl.pallas_call(
        _ada_kernel,
        grid=(n // tn,),
        in_specs=[pl.BlockSpec((r, d), lambda j: (0, 0)),
                  pl.BlockSpec((d, tn), lambda j: (0, j)),
                  pl.BlockSpec((1, tn), lambda j: (0, j))],
        out_specs=pl.BlockSpec((r, tn), lambda j: (0, j)),
        out_shape=jax.ShapeDtypeStruct((r, n), F32),
        compiler_params=_cparams("parallel"),
        name="ada",
    )(c_all, w, b.reshape(1, n))


def _mod_spec(per_row, tm, d, idx, prompt_row_block):
    if per_row:
        return pl.BlockSpec((tm, d), lambda i: (i, idx))
    return pl.BlockSpec((SUBLANES, d), lambda i: (prompt_row_block, idx))


def _ffn_kernel(*refs, per_row, with_mix, final_norm, ff, tf, datt):
    it = iter(refs)
    x_ref = next(it)
    if with_mix:
        att_ref, ys_ref, wo_ref, g2_ref = next(it), next(it), next(it), next(it)
    sh_ref, sc_ref, g_ref, nw_ref, win_ref, wout_ref = (next(it) for _ in range(6))
    nf_ref = next(it) if final_norm else None
    o_ref = next(it)
    a_ref = next(it)
    rows = slice(None) if per_row else slice(0, 1)

    x = x_ref[...]
    if with_mix:
        mix = (jnp.dot(att_ref[...].astype(BF16), wo_ref[0:datt, :], preferred_element_type=F32)
               + jnp.dot(ys_ref[...], wo_ref[datt:, :], preferred_element_type=F32))
        x = x + g2_ref[rows, :] * mix
    h = _norm_mod(x, nw_ref[...], sh_ref[rows, :], sc_ref[rows, :]).astype(BF16)
    for c in range(ff // tf):
        g = jnp.dot(h, win_ref[:, c * tf:(c + 1) * tf], preferred_element_type=F32)
        u = jnp.dot(h, win_ref[:, ff + c * tf:ff + (c + 1) * tf], preferred_element_type=F32)
        a_ref[:, c * tf:(c + 1) * tf] = (_silu(g) * u).astype(BF16)
    y = x + (0.5 * g_ref[rows, :]) * jnp.dot(a_ref[...], wout_ref[...], preferred_element_type=F32)
    if final_norm:
        y = _rms(y) * nf_ref[...]
    o_ref[...] = y


def _ffn_call(x, m_all, ada_base, norm_w, w_in_b, w_out_b, *, per_row, tm, prompt_row_block,
              mix=None, final_norm_w=None, name):
    r, d = x.shape
    ff = w_out_b.shape[0]
    tf = 256
    assert r % tm == 0 and ff % tf == 0
    const = lambda i: (0, 0)
    resident = functools.partial(pl.BlockSpec, index_map=const, pipeline_mode=pl.Buffered(1))
    mspec = functools.partial(_mod_spec, per_row, tm, d, prompt_row_block=prompt_row_block)
    args, specs = [x], [pl.BlockSpec((tm, d), lambda i: (i, 0))]
    datt = 0
    if mix is not None:
        att, ys, wo_b, g2_idx = mix
        datt = att.shape[1]
        args += [att, ys, wo_b, m_all]
        specs += [pl.BlockSpec((tm, datt), lambda i: (i, 0)),
                  pl.BlockSpec((tm, ys.shape[1]), lambda i: (i, 0)),
                  resident(wo_b.shape), mspec(g2_idx)]
    args += [m_all, m_all, m_all, norm_w.reshape(1, d), w_in_b, w_out_b]
    specs += [mspec(ada_base), mspec(ada_base + 1), mspec(ada_base + 2),
              pl.BlockSpec((1, d), const), resident(w_in_b.shape), resident(w_out_b.shape)]
    if final_norm_w is not None:
        args.append(final_norm_w.reshape(1, d))
        specs.append(pl.BlockSpec((1, d), const))
    kern = functools.partial(_ffn_kernel, per_row=per_row, with_mix=mix is not None,
                             final_norm=final_norm_w is not None, ff=ff, tf=tf, datt=datt)
    return pl.pallas_call(
        kern,
        grid=(r // tm,),
        in_specs=specs,
        out_specs=pl.BlockSpec((tm, d), lambda i: (i, 0)),
        out_shape=jax.ShapeDtypeStruct((r, d), F32),
        scratch_shapes=[pltpu.VMEM((tm, ff), BF16)],
        compiler_params=_cparams("parallel"),
        name=name,
    )(*args)


def _project(x, sh, sc, nw_ref, wqkv_ref, wzx_ref, wfd_ref, bfd_ref, *, datt, dssd, nh, nhs):
    h = _norm_mod(x, nw_ref[...], sh, sc).astype(BF16)
    qkv = jnp.dot(h, wqkv_ref[...], preferred_element_type=F32)
    zx = jnp.dot(h, wzx_ref[...], preferred_element_type=F32)
    fd = jnp.dot(h, wfd_ref[...], preferred_element_type=F32) + bfd_ref[...]
    lane = lax.broadcasted_iota(jnp.int32, (1, LANES), 1)
    logf = jnp.where(lane < nh, -_softplus(-fd[:, :LANES]), 0.0)
    dt = jnp.where(lane < nhs, _softplus(fd[:, LANES:]), 0.0)
    return (qkv[:, :datt], qkv[:, datt:2 * datt], qkv[:, 2 * datt:], zx[:, :dssd], zx[:, dssd:],
            logf, dt)


def _inproj_prompt_kernel(x_ref, sh_ref, sc_ref, nw_ref, wqkv_ref, wzx_ref, wfd_ref, bfd_ref,
                          cw_ref, cb_ref, selq_ref, qone_ref, selk_ref, kone_ref,
                          qa_ref, ka_ref, vbt_ref, kt_ref, vt_ref, lft_ref, z_ref,
                          xc_ref, dt_ref, tail_ref,
                          ptail_ref, cf_ref, *, datt, dssd, nh, nhs, hd, scale, conv_w):
    @pl.when(pl.program_id(0) == 0)
    def _():
        ptail_ref[...] = jnp.zeros_like(ptail_ref)
        cf_ref[...] = jnp.zeros_like(cf_ref)

    tm = x_ref.shape[0]
    q, k, v, z, xbc, logf, dt = _project(
        x_ref[...], sh_ref[0:1, :], sc_ref[0:1, :], nw_ref, wqkv_ref, wzx_ref, wfd_ref, bfd_ref,
        datt=datt, dssd=dssd, nh=nh, nhs=nhs)
    kt_ref[...] = k.T
    vt = v.T
    vt_ref[...] = vt
    for h in range(nh):
        vbt_ref[h] = vt[h * hd:(h + 1) * hd, :].astype(BF16)
    z_ref[...] = z
    lft_ref[...] = logf.T[0:SUBLANES, :]
    dt_ref[...] = dt

    fc = _xdot_r(_tri_matrix(tm), logf) + cf_ref[...]
    cf_ref[...] = fc[tm - 1:tm, :]

    per = LANES // hd
    lane = lax.broadcasted_iota(jnp.int32, (1, LANES), 1)
    f3 = jnp.concatenate(_split3(fc * LOG2E), axis=1)
    qs = q * (scale * LOG2E)
    qbias = (jnp.dot(f3, selq_ref[...], preferred_element_type=F32) + qone_ref[...]).astype(BF16)
    kbias = (jnp.dot(f3, selk_ref[...], preferred_element_type=F32) + kone_ref[...]).astype(BF16)
    for h in range(nh):
        g, hh = divmod(h, per)
        qpair = qs[:, g * LANES:(g + 1) * LANES]
        qa_ref[h, :, 0:LANES] = jnp.where(lane // hd == hh, qpair, 0.0).astype(BF16)
        qa_ref[h, :, LANES:2 * LANES] = qbias[:, h * LANES:(h + 1) * LANES]
    for g in range(nh // per):
        ka_ref[g, :, 0:LANES] = k[:, g * LANES:(g + 1) * LANES].astype(BF16)
        ka_ref[g, :, LANES:2 * LANES] = kbias[:, g * LANES:(g + 1) * LANES]

    ptail = ptail_ref[...]
    row8 = lax.broadcasted_iota(jnp.int32, ptail.shape, 0)
    acc = cb_ref[...]
    for i in range(conv_w):
        s = conv_w - 1 - i
        if s == 0:
            xs = xbc
        else:
            r = pltpu.roll(xbc, s, 0)
            top = jnp.where(row8 < s, pltpu.roll(ptail, s, 0), r[0:SUBLANES])
            xs = jnp.concatenate([top, r[SUBLANES:]], axis=0)
        acc = acc + xs * cw_ref[i:i + 1, :]
    xc_ref[...] = _silu(acc)
    tail = xbc[tm - SUBLANES:tm]
    ptail_ref[...] = tail
    tail_ref[...] = tail


def _bias_selectors(nh, hd):
    per = LANES // hd
    selq = np.zeros((3 * LANES, nh, LANES), np.float32)
    qone = np.zeros((nh, LANES), np.float32)
    selk = np.zeros((3 * LANES, nh // per, LANES), np.float32)
    kone = np.zeros((nh // per, LANES), np.float32)
    kone[:, 0:3] = 1.0
    for h in range(nh):
        g, hh = divmod(h, per)
        for c in range(3):
            selq[c * LANES + h, h, c] = 1.0
            selk[c * LANES + h, g, 3 + 3 * hh + c] = -1.0
        qone[h, 3 + 3 * hh:6 + 3 * hh] = 1.0
    return (jnp.asarray(selq.reshape(3 * LANES, -1), BF16), jnp.asarray(qone.reshape(1, -1), F32),
            jnp.asarray(selk.reshape(3 * LANES, -1), BF16), jnp.asarray(kone.reshape(1, -1), F32))


def _inproj_prompt_call(x, m_all, prompt_row_block, norm_w, wqkv, wzx, wfd, bfd, cw, cb, *,
                        datt, dssd, nh, nhs, hd, scale, tm):
    t, d = x.shape
    dconv = cw.shape[1]
    per = LANES // hd
    assert t % tm == 0 and nh == SUBLANES and conv_w_ok(cw.shape[0]) and 3 + 3 * per <= LANES
    const = lambda i: (0, 0)
    full = lambda a: pl.BlockSpec(a.shape, lambda i: (0,) * a.ndim)
    rowblk = lambda w: pl.BlockSpec((tm, w), lambda i: (i, 0))
    colblk = lambda h: pl.BlockSpec((h, tm), lambda i: (0, i))
    mspec = functools.partial(_mod_spec, False, tm, d, prompt_row_block=prompt_row_block)
    kern = functools.partial(_inproj_prompt_kernel, datt=datt, dssd=dssd, nh=nh, nhs=nhs, hd=hd,
                             scale=scale, conv_w=cw.shape[0])
    sel = _bias_selectors(nh, hd)
    sds = jax.ShapeDtypeStruct
    return pl.pallas_call(
        kern,
        grid=(t // tm,),
        in_specs=[rowblk(d), mspec(3), mspec(4), pl.BlockSpec((1, d), const),
                  full(wqkv), full(wzx), full(wfd), full(bfd), full(cw),
                  pl.BlockSpec((1, dconv), const)] + [full(a) for a in sel],
        out_specs=[pl.BlockSpec((nh, tm, 2 * LANES), lambda i: (0, i, 0)),
                   pl.BlockSpec((nh // per, tm, 2 * LANES), lambda i: (0, i, 0)),
                   pl.BlockSpec((nh, hd, tm), lambda i: (0, 0, i)),
                   colblk(datt), colblk(datt), colblk(SUBLANES),
                   rowblk(dssd), rowblk(dconv), rowblk(LANES),
                   pl.BlockSpec((SUBLANES, dconv), const)],
        out_shape=[sds((nh, t, 2 * LANES), BF16), sds((nh // per, t, 2 * LANES), BF16),
                   sds((nh, hd, t), BF16), sds((datt, t), F32), sds((datt, t), F32),
                   sds((SUBLANES, t), F32), sds((t, dssd), F32),
                   sds((t, dconv), F32), sds((t, LANES), F32), sds((SUBLANES, dconv), F32)],
        scratch_shapes=[pltpu.VMEM((SUBLANES, dconv), F32), pltpu.VMEM((1, LANES), F32)],
        compiler_params=_cparams("arbitrary"),
        name="inproj_prompt",
    )(x, m_all, m_all, norm_w.reshape(1, d), wqkv, wzx, wfd, bfd, cw, cb.reshape(1, dconv), *sel)


def conv_w_ok(w):
    return 1 <= w - 1 < SUBLANES


def _inproj_sample_kernel(x_ref, sh_ref, sc_ref, nw_ref, wqkv_ref, wzx_ref, wfd_ref, bfd_ref,
                          cw_ref, cb_ref, cs_ref,
                          q_ref, k_ref, v_ref, kt_ref, vt_ref, lf_ref, lft_ref, z_ref, xc_ref, dt_ref,
                          cn_ref, *, datt, dssd, nh, nhs, scale, conv_w):
    q, k, v, z, xbc, logf, dt = _project(
        x_ref[...], sh_ref[...], sc_ref[...], nw_ref, wqkv_ref, wzx_ref, wfd_ref, bfd_ref,
        datt=datt, dssd=dssd, nh=nh, nhs=nhs)
    q_ref[...] = q * scale
    k_ref[...] = k
    v_ref[...] = v
    kt_ref[...] = k.T
    vt_ref[...] = v.T
    z_ref[...] = z
    lf_ref[...] = logf[:, 0:nh]
    lft_ref[...] = logf.T[0:SUBLANES, :]
    dt_ref[...] = dt
    acc = cb_ref[...]
    for i in range(conv_w - 1):
        acc = acc + cs_ref[i] * cw_ref[i:i + 1, :]
    acc = acc + xbc * cw_ref[conv_w - 1:conv_w, :]
    xc_ref[...] = _silu(acc)
    for i in range(conv_w - 2):
        cn_ref[i] = cs_ref[i + 1]
    cn_ref[conv_w - 2] = xbc


def _inproj_sample_call(x, m_all, norm_w, wqkv, wzx, wfd, bfd, cw, cb, conv_state, *,
                        datt, dssd, nh, nhs, scale):
    r, d = x.shape
    conv_w, dconv = cw.shape
    const = lambda i: (0, 0)
    full = lambda a: pl.BlockSpec(a.shape, lambda i: (0,) * a.ndim)
    mspec = functools.partial(_mod_spec, True, r, d, prompt_row_block=0)
    kern = functools.partial(_inproj_sample_kernel, datt=datt, dssd=dssd, nh=nh, nhs=nhs,
                             scale=scale, conv_w=conv_w)
    sds = jax.ShapeDtypeStruct
    shapes = [sds((r, datt), F32), sds((r, datt), F32), sds((r, datt), F32), sds((datt, r), F32),
              sds((datt, r), F32), sds((r, nh), F32), sds((SUBLANES, r), F32), sds((r, dssd), F32),
              sds((r, dconv), F32), sds((r, LANES), F32), sds(conv_state.shape, F32)]
    return pl.pallas_call(
        kern,
        grid=(1,),
        in_specs=[full(x), mspec(3), mspec(4), pl.BlockSpec((1, d), const), full(wqkv), full(wzx),
                  full(wfd), full(bfd), full(cw), pl.BlockSpec((1, dconv), const), full(conv_state)],
        out_specs=[full(s) for s in shapes],
        out_shape=shapes,
        compiler_params=_cparams("arbitrary"),
        name="inproj_sample",
    )(x, m_all, m_all, norm_w.reshape(1, d), wqkv, wzx, wfd, bfd, cw, cb.reshape(1, dconv), conv_state)


def _attn_prompt_kernel(qi_ref, kj_ref, q_ref, k_ref, vt_ref, o_ref, m_ref, l_ref, acc_ref, *, nh, hd):
    step_id = pl.program_id(0)
    i = qi_ref[step_id]
    j = kj_ref[step_id]
    tq, tk = q_ref.shape[1], k_ref.shape[1]
    kpq = tq // tk
    per = LANES // hd

    @pl.when(j == 0)
    def _():
        m_ref[...] = jnp.full_like(m_ref, -jnp.inf)
        l_ref[...] = jnp.zeros_like(l_ref)
        acc_ref[...] = jnp.zeros_like(acc_ref)

    ones = jnp.ones((2 * SUBLANES, tk), BF16)

    def step(c0, c1, triangular):
        if triangular:
            keep = (lax.broadcasted_iota(jnp.int32, (tk, tk), 0)
                    <= lax.broadcasted_iota(jnp.int32, (tk, tk), 1))

        def scores(h):
            return lax.dot_general(k_ref[h // per], q_ref[h, c0:c1, :], (((1,), (1,)), ((), ())),
                                   preferred_element_type=F32)

        def softmax(h, st):
            if triangular:
                st = jnp.where(keep, st, -jnp.inf)
            m_old = m_ref[h, :, c0:c1]
            m_new = jnp.maximum(m_old, jnp.max(st, axis=0, keepdims=True))
            m_ref[h, :, c0:c1] = m_new
            return jnp.exp2(st - m_new).astype(BF16), jnp.exp2(m_old - m_new)

        def values(h, p, alpha):
            pv = jnp.dot(jnp.concatenate([vt_ref[h], ones], axis=0), p,
                         preferred_element_type=F32)
            acc_ref[h, :, c0:c1] = alpha * acc_ref[h, :, c0:c1] + pv[0:hd]
            l_ref[h, :, c0:c1] = alpha * l_ref[h, :, c0:c1] + pv[hd:hd + 1]

        st = {0: scores(0)}
        pa = {}
        for h in range(nh + 1):
            if h + 1 < nh:
                st[h + 1] = scores(h + 1)
            if h < nh:
                pa[h] = softmax(h, st.pop(h))
            if h >= 1:
                values(h - 1, *pa.pop(h - 1))

    @pl.when(j < kpq * i)
    def _():
        step(0, tq, False)

    for d in range(kpq):
        @pl.when(j == kpq * i + d)
        def _():
            step(d * tk, (d + 1) * tk, True)
            if d + 1 < kpq:
                step((d + 1) * tk, tq, False)

    @pl.when(j == kpq * (i + 1) - 1)
    def _():
        out_t = jnp.concatenate([acc_ref[h] * (1.0 / l_ref[h]) for h in range(nh)], axis=0)
        o_ref[...] = out_t.T.astype(BF16)


def _attn_prompt_call(qa, ka, vbt, *, nh, hd, tq, tk):
    t = qa.shape[1]
    datt = nh * hd
    per = LANES // hd
    assert t % tq == 0 and tq % tk == 0 and LANES % hd == 0 and nh % per == 0
    nq, kpq = t // tq, tq // tk
    qi = np.concatenate([np.full(kpq * (i + 1), i, np.int32) for i in range(nq)])
    kj = np.concatenate([np.arange(kpq * (i + 1), dtype=np.int32) for i in range(nq)])
    kern = functools.partial(_attn_prompt_kernel, nh=nh, hd=hd)
    return pl.pallas_call(
        kern,
        grid_spec=pltpu.PrefetchScalarGridSpec(
            num_scalar_prefetch=2,
            grid=(len(qi),),
            in_specs=[pl.BlockSpec((nh, tq, qa.shape[2]), lambda s, qi, kj: (0, qi[s], 0)),
                      pl.BlockSpec((nh // per, tk, ka.shape[2]), lambda s, qi, kj: (0, kj[s], 0)),
                      pl.BlockSpec((nh, hd, tk), lambda s, qi, kj: (0, 0, kj[s]))],
            out_specs=pl.BlockSpec((tq, datt), lambda s, qi, kj: (qi[s], 0)),
            scratch_shapes=[pltpu.VMEM((nh, 1, tq), F32), pltpu.VMEM((nh, 1, tq), F32),
                            pltpu.VMEM((nh, hd, tq), F32)]),
        out_shape=jax.ShapeDtypeStruct((t, datt), BF16),
        compiler_params=_cparams("arbitrary"),
        name="attn_prompt",
    )(jnp.asarray(qi), jnp.asarray(kj), qa, ka, vbt)


def _decode_kernel(pt_ref, q_ref, kn_ref, vn_ref, lfn_ref, ck_hbm, cv_hbm, clf_hbm, o_ref,
                   kbuf, vbuf, lfbuf, sem, qcol_ref, acc_ref, *, nh, hd, pg, layer_base):
    b = pl.program_id(0)
    n_seq = pl.num_programs(0)
    _, _, datt, page = kbuf.shape
    n_groups = pt_ref.shape[1] // pg
    streams = ((ck_hbm, kbuf), (cv_hbm, vbuf), (clf_hbm, lfbuf))

    def page_copy(k, seq, g, slot, p):
        src, dst = streams[k]
        return pltpu.make_async_copy(src.at[layer_base + pt_ref[seq, g * pg + p]], dst.at[slot, p],
                                     sem.at[slot, k])

    def start_group(seq, g, slot):
        for p in range(pg):
            for k in range(len(streams)):
                page_copy(k, seq, g, slot, p).start()

    def wait_group(seq, g, slot):
        for p in range(pg):
            for k in range(len(streams)):
                page_copy(k, seq, g, slot, p).wait()

    def head_rows(a):
        return jnp.concatenate([jnp.broadcast_to(a[h:h + 1, :], (hd, a.shape[1])) for h in range(nh)],
                               axis=0)

    @pl.when(b == 0)
    def _():
        start_group(0, 0, 0)

    qcol_ref[...] = jnp.broadcast_to(q_ref[...], (page, datt)).T
    acc_ref[...] = jnp.zeros_like(acc_ref)
    triu = (lax.broadcasted_iota(jnp.int32, (page, page), 0)
            <= lax.broadcasted_iota(jnp.int32, (page, page), 1)).astype(BF16)
    m_run = jnp.full((nh, 1), -jnp.inf, F32)
    l_run = jnp.zeros((nh, 1), F32)
    carry = jnp.zeros((nh, page), F32)

    for g in range(n_groups):
        slot = g % 2
        if g + 1 < n_groups:
            start_group(b, g + 1, 1 - slot)
        else:
            @pl.when(b + 1 < n_seq)
            def _():
                start_group(b + 1, 0, 1 - slot)
        wait_group(b, g, slot)

        f_loc = _xdot_l(jnp.concatenate([lfbuf[slot, p] for p in range(pg)], axis=0), triu)
        f_tot = jnp.broadcast_to(f_loc[:, page - 1:page], f_loc.shape)
        scores = []
        for p in range(pg):
            s = jnp.concatenate(
                [jnp.sum(kbuf[slot, p, h * hd:(h + 1) * hd, :] * qcol_ref[h * hd:(h + 1) * hd, :],
                         axis=0, keepdims=True) for h in range(nh)], axis=0)
            scores.append(s - (f_loc[p * nh:(p + 1) * nh, :] + carry))
            carry = carry + f_tot[p * nh:(p + 1) * nh, :]
        m_old = m_run
        for s in scores:
            m_run = jnp.maximum(m_run, jnp.max(s, axis=1, keepdims=True))
        alpha = jnp.exp(m_old - m_run)
        l_run = alpha * l_run
        probs = []
        for s in scores:
            pr = jnp.exp(s - m_run)
            l_run = l_run + jnp.sum(pr, axis=1, keepdims=True)
            probs.append(pr)
        for h in range(nh):
            hr = slice(h * hd, (h + 1) * hd)
            a = acc_ref[hr, :] * alpha[h:h + 1, :]
            for p in range(pg):
                a = a + vbuf[slot, p, hr, :] * probs[p][h:h + 1, :]
            acc_ref[hr, :] = a

    row = lax.broadcasted_iota(jnp.int32, (nh, datt), 0)
    col = lax.broadcasted_iota(jnp.int32, (nh, datt), 1)
    qk = jnp.where(col // hd == row, jnp.broadcast_to(q_ref[...] * kn_ref[...], (nh, datt)), 0.0)
    s_new = jnp.sum(qk, axis=1, keepdims=True) - (carry[:, 0:1] + lfn_ref[...])
    m_fin = jnp.maximum(m_run, s_new)
    a_fin = jnp.exp(m_run - m_fin)
    p_new = jnp.exp(s_new - m_fin)
    l_fin = a_fin * l_run + p_new
    tot = jnp.sum(acc_ref[...], axis=1, keepdims=True)
    vcol = jnp.broadcast_to(vn_ref[...], (LANES, datt)).T[:, 0:1]
    out = (tot * head_rows(a_fin) + head_rows(p_new.astype(BF16).astype(F32)) * vcol) \
        / head_rows(l_fin)
    o_ref[...] = jnp.broadcast_to(out, (datt, LANES)).T[0:1, :]


def _decode_call(page_table, q, kn, vn, lfn, ckt, cvt, clft, layer_base, *, nh, hd, pg):
    db, datt = q.shape
    n_pages = page_table.shape[1]
    page = ckt.shape[2]
    assert n_pages % pg == 0 and (n_pages // pg) % 2 == 0 and page == LANES
    row3 = lambda a: a.reshape(db, 1, a.shape[1])
    rowspec = lambda w: pl.BlockSpec((None, 1, w), lambda b, pt: (b, 0, 0))
    hbm = pl.BlockSpec(memory_space=pl.ANY)
    kern = functools.partial(_decode_kernel, nh=nh, hd=hd, pg=pg, layer_base=layer_base)
    out = pl.pallas_call(
        kern,
        grid_spec=pltpu.PrefetchScalarGridSpec(
            num_scalar_prefetch=1,
            grid=(db,),
            in_specs=[rowspec(datt), rowspec(datt), rowspec(datt),
                      pl.BlockSpec((None, nh, 1), lambda b, pt: (b, 0, 0)), hbm, hbm, hbm],
            out_specs=rowspec(datt),
            scratch_shapes=[pltpu.VMEM((2, pg, datt, page), F32), pltpu.VMEM((2, pg, datt, page), F32),
                            pltpu.VMEM((2, pg, nh, page), F32), pltpu.SemaphoreType.DMA((2, 3)),
                            pltpu.VMEM((datt, page), F32), pltpu.VMEM((datt, page), F32)]),
        out_shape=jax.ShapeDtypeStruct((db, 1, datt), F32),
        compiler_params=_cparams("arbitrary"),
        name="attn_sample",
    )(page_table, row3(q), row3(kn), row3(vn), lfn.reshape(db, nh, 1), ckt, cvt, clft)
    return out.reshape(db, datt)


def _ssd_prompt_kernel(xc_ref, dt_ref, z_ref, alog_ref, dskip_ref, gn_ref, y_ref, hl_ref, st_ref,
                       *, nhs, p_dim, n_state, n_groups):
    @pl.when(pl.program_id(0) == 0)
    def _():
        st_ref[...] = jnp.zeros_like(st_ref)

    for c in range(xc_ref.shape[0] // SSD_CHUNK):
        _ssd_chunk(xc_ref, dt_ref, z_ref, alog_ref, dskip_ref, gn_ref, y_ref, st_ref,
                   slice(c * SSD_CHUNK, (c + 1) * SSD_CHUNK), nhs=nhs, p_dim=p_dim,
                   n_state=n_state, n_groups=n_groups)
    hl_ref[...] = st_ref[...]


def _ssd_chunk(xc_ref, dt_ref, z_ref, alog_ref, dskip_ref, gn_ref, y_ref, st_ref, rs,
               *, nhs, p_dim, n_state, n_groups):
    ln = rs.stop - rs.start
    dssd = nhs * p_dim
    hpg = nhs // n_groups
    per = LANES // p_dim
    xs = xc_ref[rs, 0:dssd]
    dt = dt_ref[rs, :]
    a = -jnp.exp(alog_ref[...])
    expand = _expand_matrix(LANES, nhs, p_dim)
    acum = _xdot_r(_tri_matrix(ln), dt * a)
    acum_t = acum.T
    acum_x = _xdot_l(acum, expand)
    xdt = xs * _xdot_l(dt, expand)
    a_last_x = acum_x[ln - 1:ln, :]
    xdt_end = xdt * jnp.exp(a_last_x - acum_x)
    cdec = jnp.exp(acum_t[:, ln - 1:ln])
    causal = (lax.broadcasted_iota(jnp.int32, (ln, ln), 1)
              <= lax.broadcasted_iota(jnp.int32, (ln, ln), 0))
    lane = lax.broadcasted_iota(jnp.int32, (1, LANES), 1)

    y_parts = []
    for g in range(n_groups):
        bg = xc_ref[rs, dssd + g * n_state:dssd + (g + 1) * n_state]
        cg = xc_ref[rs, dssd + (n_groups + g) * n_state:dssd + (n_groups + g + 1) * n_state]
        gw = hpg * p_dim
        rows = slice(g * gw, (g + 1) * gw)
        st_g = st_ref[rows, :]
        cb = _dot_nt(cg, bg)
        y_off = _dot_nt(cg, st_g) * jnp.exp(acum_x[:, rows])
        for q in range(hpg // per):
            lsl = slice(g * gw + q * LANES, g * gw + (q + 1) * LANES)
            xdt_q = xdt[:, lsl].astype(BF16)
            y_q = jnp.zeros((ln, LANES), F32)
            for hh in range(per):
                h = g * hpg + q * per + hh
                seg = acum[:, h:h + 1] - acum_t[h:h + 1, :]
                m = cb * jnp.exp(jnp.where(causal, seg, -jnp.inf))
                yd = jnp.dot(m.astype(BF16), xdt_q, preferred_element_type=F32)
                y_q = jnp.where(lane // p_dim == hh, yd, y_q)
            y_parts.append(y_q + y_off[:, q * LANES:(q + 1) * LANES])
        cs = jnp.dot(xdt_end[:, rows].T.astype(BF16), bg.astype(BF16), preferred_element_type=F32)
        for hh in range(hpg):
            h = g * hpg + hh
            hr = slice(h * p_dim, (h + 1) * p_dim)
            dec = jnp.broadcast_to(cdec[h:h + 1, :], (p_dim, n_state))
            st_ref[hr, :] = st_ref[hr, :] * dec + cs[hh * p_dim:(hh + 1) * p_dim, :]
    y = jnp.concatenate(y_parts, axis=1)
    y_ref[rs, :] = _gated_group_norm(y, xs, z_ref[rs, :], dskip_ref[...], gn_ref[...], n_groups)


def _ssd_prompt_call(xc, dt, z, alog_pad, dskip_x, gnorm, *, nhs, p_dim, n_state, n_groups):
    t, dconv = xc.shape
    dssd = nhs * p_dim
    ln = next(c * SSD_CHUNK for c in (4, 2, 1) if t % (c * SSD_CHUNK) == 0)
    assert t % ln == 0 and LANES % p_dim == 0 and (nhs // n_groups) % (LANES // p_dim) == 0
    const = lambda i: (0, 0)
    kern = functools.partial(_ssd_prompt_kernel, nhs=nhs, p_dim=p_dim, n_state=n_state,
                             n_groups=n_groups)
    return pl.pallas_call(
        kern,
        grid=(t // ln,),
        in_specs=[pl.BlockSpec((ln, dconv), lambda i: (i, 0)),
                  pl.BlockSpec((ln, LANES), lambda i: (i, 0)),
                  pl.BlockSpec((ln, dssd), lambda i: (i, 0)),
                  pl.BlockSpec((1, LANES), const), pl.BlockSpec((1, dssd), const),
                  pl.BlockSpec((1, dssd), const)],
        out_specs=[pl.BlockSpec((ln, dssd), lambda i: (i, 0)),
                   pl.BlockSpec((dssd, n_state), const)],
        out_shape=[jax.ShapeDtypeStruct((t, dssd), BF16), jax.ShapeDtypeStruct((dssd, n_state), F32)],
        scratch_shapes=[pltpu.VMEM((dssd, n_state), F32)],
        compiler_params=_cparams("arbitrary"),
        name="ssd_prompt",
    )(xc, dt, z, alog_pad, dskip_x, gnorm)


def _ssd_sample_kernel(xc_ref, dt_ref, z_ref, alog_ref, dskip_ref, gn_ref, st_ref, y_ref, so_ref,
                       *, nhs, p_dim, n_state, n_groups):
    bb = xc_ref.shape[0]
    dssd = nhs * p_dim
    gw = dssd // n_groups
    xs = xc_ref[:, 0:dssd]
    dt = dt_ref[...]
    a = -jnp.exp(alog_ref[...])
    expand = _expand_matrix(LANES, nhs, p_dim)
    dec_x = _xdot_l(jnp.exp(dt * a), expand)
    xdt = xs * _xdot_l(dt, expand)

    def column(row):
        return jnp.broadcast_to(row, (n_state, dssd)).T

    ys = []
    for r in range(bb):
        bfull = jnp.concatenate(
            [jnp.broadcast_to(xc_ref[r:r + 1, dssd + g * n_state:dssd + (g + 1) * n_state],
                              (gw, n_state)) for g in range(n_groups)], axis=0)
        cfull = jnp.concatenate(
            [jnp.broadcast_to(xc_ref[r:r + 1, dssd + (n_groups + g) * n_state:
                                     dssd + (n_groups + g + 1) * n_state],
                              (gw, n_state)) for g in range(n_groups)], axis=0)
        s_new = st_ref[r] * column(dec_x[r:r + 1, :]) + column(xdt[r:r + 1, :]) * bfull
        so_ref[r] = s_new
        ycol = jnp.sum(s_new * cfull, axis=1, keepdims=True)
        ys.append(jnp.broadcast_to(ycol, (dssd, n_state)).T[0:1, :])
    y = jnp.concatenate(ys, axis=0)
    y_ref[...] = _gated_group_norm(y, xs, z_ref[...], dskip_ref[...], gn_ref[...], n_groups)


def _ssd_sample_call(xc, dt, z, alog_pad, dskip_x, gnorm, state, *, nhs, p_dim, n_state, n_groups):
    db, dconv = xc.shape
    dssd = nhs * p_dim
    bb = SUBLANES
    assert db % bb == 0 and n_state == LANES
    const = lambda i: (0, 0)
    kern = functools.partial(_ssd_sample_kernel, nhs=nhs, p_dim=p_dim, n_state=n_state,
                             n_groups=n_groups)
    return pl.pallas_call(
        kern,
        grid=(db // bb,),
        in_specs=[pl.BlockSpec((bb, dconv), lambda i: (i, 0)),
                  pl.BlockSpec((bb, LANES), lambda i: (i, 0)),
                  pl.BlockSpec((bb, dssd), lambda i: (i, 0)),
                  pl.BlockSpec((1, LANES), const), pl.BlockSpec((1, dssd), const),
                  pl.BlockSpec((1, dssd), const),
                  pl.BlockSpec((bb, dssd, n_state), lambda i: (i, 0, 0))],
        out_specs=[pl.BlockSpec((bb, dssd), lambda i: (i, 0)),
                   pl.BlockSpec((bb, dssd, n_state), lambda i: (i, 0, 0))],
        out_shape=[jax.ShapeDtypeStruct((db, dssd), BF16),
                   jax.ShapeDtypeStruct((db, dssd, n_state), F32)],
        compiler_params=_cparams("parallel"),
        name="ssd_sample",
    )(xc, dt, z, alog_pad, dskip_x, gnorm, state)


def _pad_lanes(v):
    return jnp.zeros((1, LANES), F32).at[0, :v.shape[0]].set(v)


def kernel(x_prompt, x_sample, cache_k, cache_v, cache_logf, state_ssm, state_conv, page_table, c_prompt, c_sample, w_ada, b_ada, norm_ffn1, w_ffn1_in, w_ffn1_out, norm_mix, w_in, b_forget, conv_w, conv_b, dt_bias, a_log, d_skip, ssd_norm, w_out, norm_ffn2, w_ffn2_in, w_ffn2_out, norm_final):
    bsz, t, d = x_prompt.shape
    db, s_new, _ = x_sample.shape
    depth, n_phys, page, nh, hd = cache_k.shape
    _, _, nhs, p_dim, n_state = state_ssm.shape
    conv_w1, dconv = state_conv.shape[2:]
    datt, dssd = nh * hd, nhs * p_dim
    n_groups = (dconv - dssd) // (2 * n_state)
    assert bsz == 1 and s_new == 1 and db % SUBLANES == 0 and c_prompt.shape[0] == 1
    scale = hd ** -0.5
    prompt_row_block = db // SUBLANES
    tm = min(512, t)
    n_pages = page_table.shape[1]
    pg = min(16, n_pages)

    xp = x_prompt.reshape(t, d)
    xs_ = x_sample.reshape(db, d)
    ckt = jnp.transpose(cache_k, (0, 1, 3, 4, 2)).reshape(depth * n_phys, datt, page)
    cvt = jnp.transpose(cache_v, (0, 1, 3, 4, 2)).reshape(depth * n_phys, datt, page)
    clft = jnp.transpose(cache_logf, (0, 1, 3, 2)).reshape(depth * n_phys, nh, page)
    heads_last = lambda at, n: jnp.transpose(at.reshape(nh, hd, n), (2, 0, 1))
    c_all = jnp.concatenate([c_sample, c_prompt, jnp.zeros((SUBLANES - 1, d), F32)], axis=0)

    outs_p, outs_s = [], []
    for l in range(depth):
        sp = (datt, 2 * datt, 3 * datt, 3 * datt + nh, 3 * datt + nh + dssd,
              3 * datt + nh + dssd + dconv)
        wi = w_in[l]
        wqkv = wi[:, :sp[2]].astype(BF16)
        wzx = wi[:, sp[3]:sp[5]].astype(BF16)
        wfd = (jnp.zeros((d, 2 * LANES), F32).at[:, :nh].set(wi[:, sp[2]:sp[3]])
               .at[:, LANES:LANES + nhs].set(wi[:, sp[5]:])).astype(BF16)
        bfd = jnp.concatenate([_pad_lanes(b_forget[l]), _pad_lanes(dt_bias[l])], axis=1)
        alog_pad = _pad_lanes(a_log[l])
        dskip_x = jnp.repeat(d_skip[l], p_dim).reshape(1, dssd)
        gnorm = ssd_norm[l].reshape(1, dssd)
        w1i, w1o = w_ffn1_in[l].astype(BF16), w_ffn1_out[l].astype(BF16)
        w2i, w2o = w_ffn2_in[l].astype(BF16), w_ffn2_out[l].astype(BF16)
        wo = w_out[l].astype(BF16)
        last = l == depth - 1

        m_all = _ada_call(c_all, w_ada[l], b_ada[l])
        proj = dict(datt=datt, dssd=dssd, nh=nh, nhs=nhs, scale=scale)
        ssd = dict(nhs=nhs, p_dim=p_dim, n_state=n_state, n_groups=n_groups)

        xp = _ffn_call(xp, m_all, 0, norm_ffn1[l], w1i, w1o, per_row=False, tm=tm,
                       prompt_row_block=prompt_row_block, name="ffn1_prompt")
        (qa, ka, vbt, kt, vt, lft, z, xc, dt, tail) = _inproj_prompt_call(
            xp, m_all, prompt_row_block, norm_mix[l], wqkv, wzx, wfd, bfd, conv_w[l], conv_b[l],
            tm=tm, hd=hd, **proj)
        att = _attn_prompt_call(qa, ka, vbt, nh=nh, hd=hd, tq=min(2 * tm, t), tk=tm)
        yssd, h_last = _ssd_prompt_call(xc, dt, z, alog_pad, dskip_x, gnorm, **ssd)
        xp = _ffn_call(xp, m_all, 6, norm_ffn2[l], w2i, w2o, per_row=False, tm=tm,
                       prompt_row_block=prompt_row_block, mix=(att, yssd, wo, 5),
                       final_norm_w=norm_final if last else None, name="ffn2_prompt")
        outs_p.append((heads_last(kt, t)[None], heads_last(vt, t)[None], lft.T[None],
                       h_last.reshape(1, nhs, p_dim, n_state),
                       tail[SUBLANES - conv_w1:].reshape(1, conv_w1, dconv)))

        xs_ = _ffn_call(xs_, m_all, 0, norm_ffn1[l], w1i, w1o, per_row=True, tm=db,
                        prompt_row_block=prompt_row_block, name="ffn1_sample")
        (q, k, v, kt, vt, logf, lft, z, xc, dt, conv_new) = _inproj_sample_call(
            xs_, m_all, norm_mix[l], wqkv, wzx, wfd, bfd, conv_w[l], conv_b[l],
            jnp.transpose(state_conv[l], (1, 0, 2)), **proj)
        att = _decode_call(page_table, q, k, v, logf, ckt, cvt, clft, l * n_phys,
                           nh=nh, hd=hd, pg=pg)
        yssd, st_new = _ssd_sample_call(xc, dt, z, alog_pad, dskip_x, gnorm,
                                        state_ssm[l].reshape(db, dssd, n_state), **ssd)
        xs_ = _ffn_call(xs_, m_all, 6, norm_ffn2[l], w2i, w2o, per_row=True, tm=db,
                        prompt_row_block=prompt_row_block, mix=(att, yssd, wo, 5),
                        final_norm_w=norm_final if last else None, name="ffn2_sample")
        outs_s.append((heads_last(kt, db)[:, None], heads_last(vt, db)[:, None], lft.T[:, None],
                       st_new.reshape(db, nhs, p_dim, n_state), jnp.transpose(conv_new, (1, 0, 2))))

    stack = lambda outs, i: jnp.stack([o[i] for o in outs])
    return (xp.reshape(bsz, t, d), xs_.reshape(db, 1, d),
            stack(outs_p, 0), stack(outs_p, 1), stack(outs_p, 2), stack(outs_p, 3), stack(outs_p, 4),
            stack(outs_s, 0), stack(outs_s, 1), stack(outs_s, 2), stack(outs_s, 3), stack(outs_s, 4))
```
